```python
import jax, jax.numpy as jnp
from jax import lax
import numpy as np

D_MODEL = 1024
BATCH = 16
SEQ = 2048
DEPTH = 2

GLA_HEADS = 4
GLA_DK = D_MODEL // 16
GLA_DV = D_MODEL // 8
GLA_RANK = 16
GLA_TAU = 16.0
GLA_CHUNK = 64
CONV_DIM = D_MODEL // 2
CONV_WIDTH = 3
SWA_HEADS = 16
SWA_KV_HEADS = 4
SWA_HEAD_DIM = D_MODEL // SWA_HEADS
WINDOW = 128
SWA_BLOCK = WINDOW
ROT_DIM = SWA_HEAD_DIM // 4
ROPE_THETA = 500000.0
XA_HEADS = 4
XA_HEAD_DIM = D_MODEL // XA_HEADS
MEM_LEN = 256
FFN_HIDDEN = -(-8 * D_MODEL // (3 * 256)) * 256

HYB_IN_WIDTH = 2 * GLA_HEADS * GLA_DK + 2 * GLA_HEADS * GLA_DV + GLA_RANK + 3 * CONV_DIM
HYB_OUT_WIDTH = GLA_HEADS * GLA_DV + CONV_DIM
SWA_QKV_WIDTH = (SWA_HEADS + 2 * SWA_KV_HEADS) * SWA_HEAD_DIM
N_EVEN = (DEPTH + 1) // 2
N_ODD = DEPTH // 2
RMS_EPS = 1e-6

kernel_name = 'hybrid_gla_shortconv_swa_sink_trunk'


def rms_norm(x, gain):
    xf = x.astype(jnp.float32)
    y = xf * lax.rsqrt(jnp.mean(xf * xf, axis=-1, keepdims=True) + RMS_EPS)
    return (y * gain.astype(jnp.float32)).astype(x.dtype)


def rope_tables(positions):
    inv_freq = ROPE_THETA ** (-jnp.arange(0, ROT_DIM, 2, dtype=jnp.float32) / ROT_DIM)
    ang = positions.astype(jnp.float32)[..., None] * inv_freq
    return jnp.cos(ang), jnp.sin(ang)


def rope_partial(x, cos, sin):
    half = ROT_DIM // 2
    x1 = x[..., :half].astype(jnp.float32)
    x2 = x[..., half:ROT_DIM].astype(jnp.float32)
    c = cos[:, :, None, :]
    s = sin[:, :, None, :]
    rot = jnp.concatenate([x1 * c - x2 * s, x2 * c + x1 * s], axis=-1).astype(x.dtype)
    return jnp.concatenate([rot, x[..., ROT_DIM:]], axis=-1)


def gla_chunked(q, k, v, log_a):
    out_dtype = v.dtype
    bsz, t, h, dk = q.shape
    dv = v.shape[-1]
    n = t // GLA_CHUNK
    f32 = jnp.float32
    q, k, v, log_a = (arr.astype(f32).reshape(bsz, n, GLA_CHUNK, h, arr.shape[-1])
                      for arr in (q, k, v, log_a))
    b = jnp.cumsum(log_a, axis=2)
    b_last = b[:, :, -1]
    q_in = q * jnp.exp(b)
    k_in = k * jnp.exp(-b)
    k_out = k * jnp.exp(b_last[:, :, None] - b)
    causal = jnp.tril(jnp.ones((GLA_CHUNK, GLA_CHUNK), dtype=bool))
    scores = jnp.einsum('bnihd,bnjhd->bnhij', q_in, k_in)
    scores = jnp.where(causal, scores, 0.0)
    o_intra = jnp.einsum('bnhij,bnjhv->bnihv', scores, v)
    kv = jnp.einsum('bnjhd,bnjhv->nbhdv', k_out, v)
    decay = jnp.exp(b_last).transpose(1, 0, 2, 3)

    def step(state, inp):
        kv_c, d_c = inp
        return d_c[..., None] * state + kv_c, state

    _, s_prev = lax.scan(step, jnp.zeros((bsz, h, dk, dv), f32), (kv, decay))
    o_inter = jnp.einsum('bnihd,nbhdv->bnihv', q_in, s_prev)
    return (o_intra + o_inter).reshape(bsz, t, h, dv).astype(out_dtype)


def gla_shortconv_mixer(h, w_in, w_gate2, gate_bias, gla_out_gain, conv_w, w_out):
    bsz, t, _ = h.shape
    widths = [GLA_HEADS * GLA_DK, GLA_HEADS * GLA_DK, GLA_HEADS * GLA_DV, GLA_RANK,
              GLA_HEADS * GLA_DV, CONV_DIM, CONV_DIM, CONV_DIM]
    cuts = [int(c) for c in np.cumsum(widths)[:-1]]
    q, k, v, g_low, g_out, conv_b, conv_c, conv_in = jnp.split(h @ w_in, cuts, axis=-1)
    q = q.reshape(bsz, t, GLA_HEADS, GLA_DK) * (GLA_DK ** -0.5)
    k = k.reshape(bsz, t, GLA_HEADS, GLA_DK)
    v = v.reshape(bsz, t, GLA_HEADS, GLA_DV)
    gate_logits = (g_low @ w_gate2 + gate_bias).astype(jnp.float32)
    log_a = (jax.nn.log_sigmoid(gate_logits) / GLA_TAU).reshape(bsz, t, GLA_HEADS, GLA_DK)
    o = gla_chunked(q, k, v, log_a)
    o = rms_norm(o, gla_out_gain) * jax.nn.silu(g_out).reshape(bsz, t, GLA_HEADS, GLA_DV)
    y_gla = o.reshape(bsz, t, GLA_HEADS * GLA_DV)
    u = conv_c * conv_in
    u = lax.conv_general_dilated(u, conv_w[:, None, :], window_strides=(1,),
                                 padding=[(CONV_WIDTH - 1, 0)],
                                 dimension_numbers=('NWC', 'WIO', 'NWC'),
                                 feature_group_count=CONV_DIM)
    y_conv = conv_b * u
    return jnp.concatenate([y_gla, y_conv], axis=-1) @ w_out


def banded_sink_attention(q, k, v, sinks):
    bsz, t, hq, hd = q.shape
    hkv = k.shape[2]
    g = hq // hkv
    w = SWA_BLOCK
    n = t // w
    qb = q.reshape(bsz, n, w, hkv, g, hd).transpose(1, 0, 2, 3, 4, 5)
    kb = k.reshape(bsz, n, w, hkv, hd).transpose(1, 0, 2, 3, 4)
    vb = v.reshape(bsz, n, w, hkv, hd).transpose(1, 0, 2, 3, 4)
    kk = jnp.concatenate([jnp.concatenate([jnp.zeros_like(kb[:1]), kb[:-1]], axis=0), kb], axis=2)
    vv = jnp.concatenate([jnp.concatenate([jnp.zeros_like(vb[:1]), vb[:-1]], axis=0), vb], axis=2)
    qi = jnp.arange(w)[:, None] + w
    si = jnp.arange(2 * w)[None, :]
    band = (qi - si >= 0) & (qi - si < WINDOW)
    sink = sinks.astype(jnp.float32).reshape(hkv, g)[None, :, :, None, None]
    scale = hd ** -0.5

    def block(args):
        qn, kn, vn, idx = args
        s = jnp.einsum('bqkgd,bskd->bkgqs', qn, kn).astype(jnp.float32) * scale
        valid = band & ((si >= w) | (idx > 0))
        s = jnp.where(valid, s, -jnp.inf)
        m = jnp.maximum(jnp.max(s, axis=-1, keepdims=True), sink)
        p = jnp.exp(s - m)
        p = p / (jnp.sum(p, axis=-1, keepdims=True) + jnp.exp(sink - m))
        return jnp.einsum('bkgqs,bskd->bqkgd', p.astype(vn.dtype), vn)

    o = lax.map(block, (qb, kk, vv, jnp.arange(n)))
    return o.transpose(1, 0, 2, 3, 4, 5).reshape(bsz, t, hq, hd)


def swa_sink_mixer(h, cos, sin, w_qkv, q_gain, k_gain, sinks, w_out):
    bsz, t, _ = h.shape
    cq = SWA_HEADS * SWA_HEAD_DIM
    ck = cq + SWA_KV_HEADS * SWA_HEAD_DIM
    q, k, v = jnp.split(h @ w_qkv, [cq, ck], axis=-1)
    q = q.reshape(bsz, t, SWA_HEADS, SWA_HEAD_DIM)
    k = k.reshape(bsz, t, SWA_KV_HEADS, SWA_HEAD_DIM)
    v = v.reshape(bsz, t, SWA_KV_HEADS, SWA_HEAD_DIM)
    q = rope_partial(rms_norm(q, q_gain), cos, sin)
    k = rope_partial(rms_norm(k, k_gain), cos, sin)
    o = banded_sink_attention(q, k, v, sinks)
    return o.reshape(bsz, t, cq) @ w_out


def memory_cross_attention(h, mem_h, wq, wkv, q_gain, k_gain, wo):
    bsz, t, d = h.shape
    m = mem_h.shape[1]
    q = (h @ wq).reshape(bsz, t, XA_HEADS, XA_HEAD_DIM)
    k, v = jnp.split(mem_h @ wkv, 2, axis=-1)
    k = k.reshape(bsz, m, XA_HEADS, XA_HEAD_DIM)
    v = v.reshape(bsz, m, XA_HEADS, XA_HEAD_DIM)
    q = rms_norm(q, q_gain)
    k = rms_norm(k, k_gain)
    s = jnp.einsum('bthd,bmhd->bhtm', q, k).astype(jnp.float32) * (XA_HEAD_DIM ** -0.5)
    p = jax.nn.softmax(s, axis=-1)
    o = jnp.einsum('bhtm,bmhd->bthd', p.astype(v.dtype), v)
    return o.reshape(bsz, t, d) @ wo


def swiglu(h, w_gate_up, w_down):
    gate, up = jnp.split(h @ w_gate_up, 2, axis=-1)
    return (jax.nn.silu(gate) * up) @ w_down


def setup_inputs(seed: int = 0) -> dict:
    key = jax.random.key(seed)
    ks = iter(jax.random.split(key, 32))
    f32 = jnp.float32

    def dense(shape, fan_in):
        return jax.random.normal(next(ks), shape, f32) * fan_in ** -0.5

    def gain(shape):
        return 1.0 + 0.02 * jax.random.normal(next(ks), shape, f32)

    x = jax.random.normal(next(ks), (BATCH, SEQ, D_MODEL), f32)
    mem = jax.random.normal(next(ks), (BATCH, MEM_LEN, D_MODEL), f32)
    offsets = jax.random.randint(next(ks), (BATCH, 1), 0, 1024, dtype=jnp.int32)
    positions = offsets + jnp.arange(SEQ, dtype=jnp.int32)[None, :]
    return {
        'x': x,
        'mem': mem,
        'positions': positions,
        'mix_norm': gain((DEPTH, D_MODEL)),
        'hyb_w_in': dense((N_EVEN, D_MODEL, HYB_IN_WIDTH), D_MODEL),
        'gla_w_gate2': dense((N_EVEN, GLA_RANK, GLA_HEADS * GLA_DK), GLA_RANK),
        'gla_gate_bias': 0.1 * jax.random.normal(next(ks), (N_EVEN, GLA_HEADS * GLA_DK), f32),
        'gla_out_gain': gain((N_EVEN, GLA_DV)),
        'conv_w': dense((N_EVEN, CONV_WIDTH, CONV_DIM), CONV_WIDTH),
        'hyb_w_out': dense((N_EVEN, HYB_OUT_WIDTH, D_MODEL), HYB_OUT_WIDTH),
        'swa_w_qkv': dense((N_ODD, D_MODEL, SWA_QKV_WIDTH), D_MODEL),
        'swa_q_gain': gain((N_ODD, SWA_HEAD_DIM)),
        'swa_k_gain': gain((N_ODD, SWA_HEAD_DIM)),
        'swa_sinks': jax.random.normal(next(ks), (N_ODD, SWA_HEADS), f32),
        'swa_w_out': dense((N_ODD, SWA_HEADS * SWA_HEAD_DIM, D_MODEL), SWA_HEADS * SWA_HEAD_DIM),
        'mem_norm': gain((DEPTH, D_MODEL)),
        'xa_norm': gain((DEPTH, D_MODEL)),
        'xa_wq': dense((DEPTH, D_MODEL, D_MODEL), D_MODEL),
        'xa_wkv': dense((DEPTH, D_MODEL, 2 * D_MODEL), D_MODEL),
        'xa_q_gain': gain((DEPTH, XA_HEAD_DIM)),
        'xa_k_gain': gain((DEPTH, XA_HEAD_DIM)),
        'xa_wo': dense((DEPTH, D_MODEL, D_MODEL), D_MODEL),
        'ffn_norm': gain((DEPTH, D_MODEL)),
        'ffn_w_gate_up': dense((DEPTH, D_MODEL, 2 * FFN_HIDDEN), D_MODEL),
        'ffn_w_down': dense((DEPTH, FFN_HIDDEN, D_MODEL), FFN_HIDDEN),
    }


def reference(x, mem, positions, mix_norm, hyb_w_in, gla_w_gate2, gla_gate_bias, gla_out_gain,
              conv_w, hyb_w_out, swa_w_qkv, swa_q_gain, swa_k_gain, swa_sinks, swa_w_out,
              mem_norm, xa_norm, xa_wq, xa_wkv, xa_q_gain, xa_k_gain, xa_wo,
              ffn_norm, ffn_w_gate_up, ffn_w_down):
    cos, sin = rope_tables(positions)
    for layer in range(DEPTH):
        i = layer // 2
        h = rms_norm(x, mix_norm[layer])
        if layer % 2 == 0:
            x = x + gla_shortconv_mixer(h, hyb_w_in[i], gla_w_gate2[i], gla_gate_bias[i],
                                        gla_out_gain[i], conv_w[i], hyb_w_out[i])
        else:
            x = x + swa_sink_mixer(h, cos, sin, swa_w_qkv[i], swa_q_gain[i], swa_k_gain[i],
                                   swa_sinks[i], swa_w_out[i])
        mem_h = rms_norm(mem, mem_norm[layer])
        x = x + memory_cross_attention(rms_norm(x, xa_norm[layer]), mem_h, xa_wq[layer],
                                       xa_wkv[layer], xa_q_gain[layer], xa_k_gain[layer],
                                       xa_wo[layer])
        x = x + swiglu(rms_norm(x, ffn_norm[layer]), ffn_w_gate_up[layer], ffn_w_down[layer])
    return x
```

```python
import functools

import numpy as np
import jax
import jax.numpy as jnp
from jax import lax
from jax.experimental import pallas as pl
from jax.experimental.pallas import tpu as pltpu

F32 = jnp.float32
BF16 = jnp.bfloat16

D_MODEL = 1024
RMS_EPS = 1e-6
GLA_HEADS = 4
GLA_DK = 64
GLA_DV = 128
GLA_RANK = 16
GLA_TAU = 16.0
GLA_CHUNK = 64
GLA_KW = GLA_HEADS * GLA_DK
GLA_VW = GLA_HEADS * GLA_DV
CONV_DIM = 512
CONV_WIDTH = 3
RANK_PAD = 128
SWA_HEADS = 16
SWA_KV_HEADS = 4
SWA_GROUP = SWA_HEADS // SWA_KV_HEADS
SWA_HEAD_DIM = 64
WINDOW = 128
ROT_DIM = 16
ROPE_THETA = 500000.0
SWA_GW = SWA_GROUP * SWA_HEAD_DIM
XA_HEADS = 4
XA_HEAD_DIM = 256
FFN_HIDDEN = 2816
FFN_CHUNK = 256

LANES = 128
SUBLANES = 8
VMEM_LIMIT = 56 * 1024 * 1024

GLA_TM = 256
SWA_TQ = 256
XA_TM = 512
FFN_TM = 512


def _rms(x, gain):
    return x * lax.rsqrt(jnp.mean(x * x, axis=-1, keepdims=True) + RMS_EPS) * gain


def _dot(a, b):
    return jnp.dot(a, b, preferred_element_type=F32)


def _dot_nt(a, b):
    return lax.dot_general(a, b, (((1,), (1,)), ((), ())), preferred_element_type=F32)


def _dot_tn(a, b):
    return lax.dot_general(a, b, (((0,), (0,)), ((), ())), preferred_element_type=F32)


def _split_bf16(x):
    hi = x.astype(BF16)
    lo = (x - hi.astype(F32)).astype(BF16)
    return hi, lo


def _const_spec(shape):
    return pl.BlockSpec(shape, lambda *_: (0,) * len(shape))


def _params(*semantics):
    return pltpu.CompilerParams(dimension_semantics=semantics, vmem_limit_bytes=VMEM_LIMIT)


def _ffn_body(x_ref, gain_ref, wg_ref, wu_ref, wd_ref, o_ref, h_ref, acc_ref):
    x = x_ref[...]
    h_ref[...] = _rms(x, gain_ref[...]).astype(BF16)
    acc_ref[...] = x
    for c in range(FFN_HIDDEN // FFN_CHUNK):
        h = h_ref[...]
        g = _dot(h, wg_ref[c])
        u = _dot(h, wu_ref[c])
        a = (g / (1.0 + jnp.exp(-g)) * u).astype(BF16)
        acc_ref[...] += _dot(a, wd_ref[c])
    o_ref[...] = acc_ref[...]


def _ffn(x2, gain, w_gate_up, w_down):
    m, d = x2.shape
    nc = FFN_HIDDEN // FFN_CHUNK
    wg = w_gate_up[:, :FFN_HIDDEN].reshape(d, nc, FFN_CHUNK).transpose(1, 0, 2).astype(BF16)
    wu = w_gate_up[:, FFN_HIDDEN:].reshape(d, nc, FFN_CHUNK).transpose(1, 0, 2).astype(BF16)
    wd = w_down.reshape(nc, FFN_CHUNK, d).astype(BF16)
    tm = min(FFN_TM, m)
    return pl.pallas_call(
        _ffn_body,
        out_shape=jax.ShapeDtypeStruct((m, d), F32),
        grid=(m // tm,),
        in_specs=[
            pl.BlockSpec((tm, d), lambda i: (i, 0)),
            _const_spec((1, d)),
            _const_spec((nc, d, FFN_CHUNK)),
            _const_spec((nc, d, FFN_CHUNK)),
            _const_spec((nc, FFN_CHUNK, d)),
        ],
        out_specs=pl.BlockSpec((tm, d), lambda i: (i, 0)),
        scratch_shapes=[pltpu.VMEM((tm, d), BF16), pltpu.VMEM((tm, d), F32)],
        compiler_params=_params("parallel"),
        name="ffn",
    )(x2, gain.reshape(1, d), wg, wu, wd)


def _xa_kv_body(mem_ref, gain_ref, wkv_ref, kgain_ref, k_ref, v_ref):
    mh = _rms(mem_ref[0], gain_ref[...]).astype(BF16)
    kv = _dot(mh, wkv_ref[...])
    for h in range(XA_HEADS):
        lo = h * XA_HEAD_DIM
        k_ref[0, :, lo:lo + XA_HEAD_DIM] = _rms(kv[:, lo:lo + XA_HEAD_DIM], kgain_ref[...]).astype(BF16)
    v_ref[0] = kv[:, D_MODEL:].astype(BF16)


def _xa_body(x_ref, gain_ref, wq_ref, qgain_ref, k_ref, v_ref, wo_ref, o_ref, att_ref):
    x = x_ref[0]
    h = _rms(x, gain_ref[...]).astype(BF16)
    q = _dot(h, wq_ref[...])
    scale = XA_HEAD_DIM ** -0.5
    for hd in range(XA_HEADS):
        lo = hd * XA_HEAD_DIM
        qh = (_rms(q[:, lo:lo + XA_HEAD_DIM], qgain_ref[...]) * scale).astype(BF16)
        s = _dot_nt(qh, k_ref[0, :, lo:lo + XA_HEAD_DIM])
        p = jnp.exp(s - jnp.max(s, axis=-1, keepdims=True))
        p = p / jnp.sum(p, axis=-1, keepdims=True)
        att_ref[:, lo:lo + XA_HEAD_DIM] = _dot(p.astype(BF16), v_ref[0, :, lo:lo + XA_HEAD_DIM]).astype(BF16)
    o_ref[0] = x + _dot(att_ref[...], wo_ref[...])


def _cross_attention(x, mem, mem_gain, xa_gain, wq, wkv, q_gain, k_gain, wo):
    b, t, d = x.shape
    m = mem.shape[1]
    k, v = pl.pallas_call(
        _xa_kv_body,
        out_shape=(jax.ShapeDtypeStruct((b, m, d), BF16), jax.ShapeDtypeStruct((b, m, d), BF16)),
        grid=(b,),
        in_specs=[
            pl.BlockSpec((1, m, d), lambda i: (i, 0, 0)),
            _const_spec((1, d)),
            _const_spec((d, 2 * d)),
            _const_spec((1, XA_HEAD_DIM)),
        ],
        out_specs=(pl.BlockSpec((1, m, d), lambda i: (i, 0, 0)), pl.BlockSpec((1, m, d), lambda i: (i, 0, 0))),
        compiler_params=_params("parallel"),
        name="xa_kv",
    )(mem, mem_gain.reshape(1, d), wkv.astype(BF16), k_gain.reshape(1, XA_HEAD_DIM))
    tm = min(XA_TM, t)
    return pl.pallas_call(
        _xa_body,
        out_shape=jax.ShapeDtypeStruct((b, t, d), F32),
        grid=(b, t // tm),
        in_specs=[
            pl.BlockSpec((1, tm, d), lambda i, j: (i, j, 0)),
            _const_spec((1, d)),
            _const_spec((d, d)),
            _const_spec((1, XA_HEAD_DIM)),
            pl.BlockSpec((1, m, d), lambda i, j: (i, 0, 0)),
            pl.BlockSpec((1, m, d), lambda i, j: (i, 0, 0)),
            _const_spec((d, d)),
        ],
        out_specs=pl.BlockSpec((1, tm, d), lambda i, j: (i, j, 0)),
        scratch_shapes=[pltpu.VMEM((tm, d), BF16)],
        compiler_params=_params("parallel", "parallel"),
        name="xa",
    )(x, xa_gain.reshape(1, d), wq.astype(BF16), q_gain.reshape(1, XA_HEAD_DIM), k, v, wo.astype(BF16))


_GLA_Q, _GLA_K, _GLA_V, _GLA_GO, _GLA_CB, _GLA_CC, _GLA_CI, _GLA_GL = (
    0, 256, 512, 1024, 1536, 2048, 2560, 3072)
GLA_PROJ_W = _GLA_GL + RANK_PAD


def _gla_body(x_ref, gain_ref, win_ref, wg2_ref, gbias_ref, ogain_ref, convw_ref, wout_ref,
              tri_ref, ones_ref, hmask_ref, smask_ref, o_ref,
              st_ref, ubuf_ref, cat_ref):
    tm = x_ref.shape[1]
    first = pl.program_id(1) == 0

    @pl.when(first)
    def _():
        st_ref[...] = jnp.zeros_like(st_ref)
        ubuf_ref[0:SUBLANES, :] = jnp.zeros((SUBLANES, CONV_DIM), F32)

    x = x_ref[0]
    h = _rms(x, gain_ref[...]).astype(BF16)
    proj = _dot(h, win_ref[...])

    logits = _dot(proj[:, _GLA_GL:_GLA_GL + RANK_PAD].astype(BF16), wg2_ref[...]) + gbias_ref[...]
    log_a = (jnp.minimum(logits, 0.0) - jnp.log(1.0 + jnp.exp(-jnp.abs(logits)))) * (1.0 / GLA_TAU)
    la_hi, la_lo = _split_bf16(log_a)
    b = _dot(tri_ref[...], la_hi) + _dot(tri_ref[...], la_lo)
    b_last = _dot(ones_ref[...], la_hi) + _dot(ones_ref[...], la_lo)

    q = proj[:, _GLA_Q:_GLA_Q + GLA_KW] * (GLA_DK ** -0.5)
    k = proj[:, _GLA_K:_GLA_K + GLA_KW]
    v = proj[:, _GLA_V:_GLA_V + GLA_VW].astype(BF16)
    q_in = (q * jnp.exp(b)).astype(BF16)
    k_in = (k * jnp.exp(-b)).astype(BF16)
    k_out = (k * jnp.exp(b_last - b)).astype(BF16)
    decay = jnp.exp(b_last)

    n_chunks = tm // GLA_CHUNK
    o_inter = []
    for c in range(n_chunks):
        r0 = c * GLA_CHUNK
        st = st_ref[...]
        o_inter.append(_dot_nt(q_in[r0:r0 + GLA_CHUNK], st.astype(BF16)))
        kv_t = _dot_tn(v[r0:r0 + GLA_CHUNK], k_out[r0:r0 + GLA_CHUNK])
        st_ref[...] = st * decay[r0:r0 + 1, :] + jnp.where(smask_ref[...] > 0.5, kv_t, 0.0)
    o_inter = jnp.concatenate(o_inter, axis=0)

    causal = tri_ref[...] > 0.5
    g_out = proj[:, _GLA_GO:_GLA_GO + GLA_VW]
    for hd in range(GLA_HEADS):
        lo = hd * GLA_DV
        q_h = jnp.where(hmask_ref[hd:hd + 1, :] > 0.5, q_in, jnp.zeros_like(q_in))
        scores = jnp.where(causal, _dot_nt(q_h, k_in), 0.0)
        o_h = _dot(scores.astype(BF16), v[:, lo:lo + GLA_DV]) + o_inter[:, lo:lo + GLA_DV]
        g_h = g_out[:, lo:lo + GLA_DV]
        y_h = _rms(o_h, ogain_ref[...]) * (g_h / (1.0 + jnp.exp(-g_h)))
        cat_ref[:, lo:lo + GLA_DV] = y_h.astype(BF16)

    u = proj[:, _GLA_CC:_GLA_CC + CONV_DIM] * proj[:, _GLA_CI:_GLA_CI + CONV_DIM]
    ubuf_ref[SUBLANES:SUBLANES + tm, :] = u
    y = (convw_ref[0:1, :] * ubuf_ref[SUBLANES - 2:SUBLANES - 2 + tm, :]
         + convw_ref[1:2, :] * ubuf_ref[SUBLANES - 1:SUBLANES - 1 + tm, :]
         + convw_ref[2:3, :] * u)
    cat_ref[:, GLA_VW:GLA_VW + CONV_DIM] = (proj[:, _GLA_CB:_GLA_CB + CONV_DIM] * y).astype(BF16)
    ubuf_ref[0:SUBLANES, :] = ubuf_ref[tm:tm + SUBLANES, :]

    o_ref[0] = x + _dot(cat_ref[...], wout_ref[...])


def _gla_layer(x, gain, w_in, w_gate2, gate_bias, out_gain, conv_w, w_out):
    b, t, d = x.shape
    tm = min(GLA_TM, t)
    widths = [GLA_KW, GLA_KW, GLA_VW, GLA_RANK, GLA_VW, CONV_DIM, CONV_DIM, CONV_DIM]
    cuts = [int(c) for c in np.cumsum(widths)[:-1]]
    wq, wk, wv, wgl, wgo, wcb, wcc, wci = jnp.split(w_in, cuts, axis=1)
    wgl = jnp.pad(wgl, ((0, 0), (0, RANK_PAD - GLA_RANK)))
    w_in_r = jnp.concatenate([wq, wk, wv, wgo, wcb, wcc, wci, wgl], axis=1).astype(BF16)
    wg2 = jnp.pad(w_gate2, ((0, RANK_PAD - GLA_RANK), (0, 0))).astype(BF16)

    r = np.arange(tm)
    same_chunk = (r[:, None] // GLA_CHUNK) == (r[None, :] // GLA_CHUNK)
    tri = jnp.asarray(same_chunk & (r[None, :] <= r[:, None]), BF16)
    ones = jnp.asarray(same_chunk, BF16)
    hmask = jnp.asarray(np.arange(GLA_KW)[None, :] // GLA_DK == np.arange(GLA_HEADS)[:, None], F32)
    smask = jnp.asarray(
        (np.arange(GLA_VW)[:, None] // GLA_DV) == (np.arange(GLA_KW)[None, :] // GLA_DK), F32)

    return pl.pallas_call(
        _gla_body,
        out_shape=jax.ShapeDtypeStruct((b, t, d), F32),
        grid=(b, t // tm),
        in_specs=[
            pl.BlockSpec((1, tm, d), lambda i, j: (i, j, 0)),
            _const_spec((1, d)),
            _const_spec((d, GLA_PROJ_W)),
            _const_spec((RANK_PAD, GLA_KW)),
            _const_spec((1, GLA_KW)),
            _const_spec((1, GLA_DV)),
            _const_spec((CONV_WIDTH, CONV_DIM)),
            _const_spec((GLA_VW + CONV_DIM, d)),
            _const_spec((tm, tm)),
            _const_spec((tm, tm)),
            _const_spec((GLA_HEADS, GLA_KW)),
            _const_spec((GLA_VW, GLA_KW)),
        ],
        out_specs=pl.BlockSpec((1, tm, d), lambda i, j: (i, j, 0)),
        scratch_shapes=[
            pltpu.VMEM((GLA_VW, GLA_KW), F32),
            pltpu.VMEM((tm + SUBLANES, CONV_DIM), F32),
            pltpu.VMEM((tm, GLA_VW + CONV_DIM), BF16),
        ],
        compiler_params=_params("parallel", "arbitrary"),
        name="gla_conv",
    )(x, gain.reshape(1, d), w_in_r, wg2, gate_bias.reshape(1, GLA_KW), out_gain.reshape(1, GLA_DV),
      conv_w, w_out.astype(BF16), tri, ones, hmask, smask)


def _head_norm(z, gmat_ref, gain):
    zz = z * z
    hi, lo = _split_bf16(zz)
    cols = []
    for c in range(z.shape[1] // SWA_GW):
        sl = slice(c * SWA_GW, (c + 1) * SWA_GW)
        cols.append(_dot(hi[:, sl], gmat_ref[...]) + _dot(lo[:, sl], gmat_ref[...]))
    ss = cols[0] if len(cols) == 1 else jnp.concatenate(cols, axis=1)
    return z * lax.rsqrt(ss * (1.0 / SWA_HEAD_DIM) + RMS_EPS) * gain


def _rope(z, cos_t, sin_lo, sin_hi):
    cols = []
    for c in range(z.shape[1] // LANES):
        zc = z[:, c * LANES:(c + 1) * LANES]
        cols.append(zc * cos_t
                    + pltpu.roll(zc, LANES - ROT_DIM // 2, axis=1) * sin_lo
                    + pltpu.roll(zc, ROT_DIM // 2, axis=1) * sin_hi)
    return jnp.concatenate(cols, axis=1)


def _dup_head(z, g, lane_lt64):
    zc = z[:, (g // 2) * LANES:(g // 2 + 1) * LANES]
    rc = pltpu.roll(zc, SWA_HEAD_DIM, axis=1)
    return jnp.where(lane_lt64, zc, rc) if g % 2 == 0 else jnp.where(lane_lt64, rc, zc)


def _swa_body(sinks_ref, x_ref, gain_ref, wqkv_ref, qgain_ref, kgain_ref, cos_ref, slo_ref, shi_ref,
              gmat_ref, band_ref, band0_ref, wout_ref, o_ref,
              q_ref, kbuf_ref, vbuf_ref, att_ref):
    tq = x_ref.shape[1]
    nb = tq // WINDOW
    n = pl.program_id(1)

    @pl.when(n == 0)
    def _():
        kbuf_ref[:, 0:WINDOW, :] = jnp.zeros((SWA_KV_HEADS, WINDOW, SWA_GW), BF16)
        vbuf_ref[:, :, 0:WINDOW, :] = jnp.zeros((SWA_KV_HEADS, SWA_GROUP, WINDOW, SWA_GW), BF16)

    x = x_ref[0]
    h = _rms(x, gain_ref[...]).astype(BF16)
    qkv = _dot(h, wqkv_ref[...])
    qw = SWA_HEADS * SWA_HEAD_DIM
    kw = SWA_KV_HEADS * SWA_HEAD_DIM
    cos_t, sin_lo, sin_hi = cos_ref[0], slo_ref[0], shi_ref[0]
    q = _rope(_head_norm(qkv[:, :qw], gmat_ref, qgain_ref[...]), cos_t, sin_lo, sin_hi)
    q = (q * (SWA_HEAD_DIM ** -0.5)).astype(BF16)
    k = _rope(_head_norm(qkv[:, qw:qw + kw], gmat_ref, kgain_ref[...]), cos_t, sin_lo, sin_hi)
    v = qkv[:, qw + kw:qw + 2 * kw]

    lane = lax.broadcasted_iota(jnp.int32, (1, LANES), 1)
    lane_lt64 = lane < SWA_HEAD_DIM
    zeros128 = jnp.zeros((tq, LANES), BF16)
    for g in range(SWA_KV_HEADS):
        q_ref[g] = q[:, g * SWA_GW:(g + 1) * SWA_GW]
        kd = _dup_head(k, g, lane_lt64).astype(BF16)
        kbuf_ref[g, WINDOW:WINDOW + tq, :] = jnp.concatenate([kd, kd], axis=1)
        vd = _dup_head(v, g, lane_lt64).astype(BF16)
        v_lo = jnp.where(lane_lt64, vd, zeros128)
        v_hi = jnp.where(lane_lt64, zeros128, vd)
        vbuf_ref[g, 0, WINDOW:WINDOW + tq, :] = jnp.concatenate([v_lo, zeros128], axis=1)
        vbuf_ref[g, 1, WINDOW:WINDOW + tq, :] = jnp.concatenate([v_hi, zeros128], axis=1)
        vbuf_ref[g, 2, WINDOW:WINDOW + tq, :] = jnp.concatenate([zeros128, v_lo], axis=1)
        vbuf_ref[g, 3, WINDOW:WINDOW + tq, :] = jnp.concatenate([zeros128, v_hi], axis=1)

    head_lane = lax.broadcasted_iota(jnp.int32, (1, SWA_GW), 1)
    bias0 = jnp.where(n > 0, band_ref[...], band0_ref[...])

    def group(g, carry):
        for i in range(nb):
            r0 = i * WINDOW
            bias = bias0 if i == 0 else band_ref[...]
            qg = q_ref[g, r0:r0 + WINDOW, :]
            kk = kbuf_ref[g, r0:r0 + 2 * WINDOW, :]
            acc = jnp.zeros((WINDOW, SWA_GW), F32)
            for j in range(SWA_GROUP):
                own = (head_lane >= j * SWA_HEAD_DIM) & (head_lane < (j + 1) * SWA_HEAD_DIM)
                s = _dot_nt(jnp.where(own, qg, jnp.zeros_like(qg)), kk) + bias
                sink = sinks_ref[g * SWA_GROUP + j]
                m = jnp.maximum(jnp.max(s, axis=-1, keepdims=True), sink)
                p = jnp.exp(s - m)
                p = p / (jnp.sum(p, axis=-1, keepdims=True) + jnp.exp(sink - m))
                acc = acc + _dot(p.astype(BF16), vbuf_ref[g, j, r0:r0 + 2 * WINDOW, :])
            att_ref[g, r0:r0 + WINDOW, :] = acc.astype(BF16)
        return carry

    lax.fori_loop(0, SWA_KV_HEADS, group, 0)

    att = jnp.concatenate([att_ref[g] for g in range(SWA_KV_HEADS)], axis=1)
    o_ref[0] = x + _dot(att, wout_ref[...])

    kbuf_ref[:, 0:WINDOW, :] = kbuf_ref[:, tq:tq + WINDOW, :]
    vbuf_ref[:, :, 0:WINDOW, :] = vbuf_ref[:, :, tq:tq + WINDOW, :]


def _rope_lane_tables(positions):
    inv_freq = ROPE_THETA ** (-jnp.arange(0, ROT_DIM, 2, dtype=F32) / ROT_DIM)
    ang = positions.astype(F32)[..., None] * inv_freq
    cos, sin = jnp.cos(ang), jnp.sin(ang)
    lane_in_head = np.arange(LANES) % SWA_HEAD_DIM
    freq = lane_in_head % (ROT_DIM // 2)
    cos_l, sin_l = cos[..., freq], sin[..., freq]
    cos_t = jnp.where(lane_in_head < ROT_DIM, cos_l, 1.0)
    sin_lo = jnp.where(lane_in_head < ROT_DIM // 2, -sin_l, 0.0)
    sin_hi = jnp.where((lane_in_head >= ROT_DIM // 2) & (lane_in_head < ROT_DIM), sin_l, 0.0)
    return cos_t, sin_lo, sin_hi


def _swa_layer(x, positions, gain, w_qkv, q_gain, k_gain, sinks, w_out):
    b, t, d = x.shape
    tq = min(SWA_TQ, t)
    cos_t, sin_lo, sin_hi = _rope_lane_tables(positions)
    qgain_t = jnp.tile(q_gain, d // SWA_HEAD_DIM).reshape(1, d)
    kw = SWA_KV_HEADS * SWA_HEAD_DIM
    kgain_t = jnp.tile(k_gain, SWA_KV_HEADS).reshape(1, kw)
    lanes = np.arange(SWA_GW)
    gmat = jnp.asarray(lanes[:, None] // SWA_HEAD_DIM == lanes[None, :] // SWA_HEAD_DIM, BF16)
    qi = np.arange(WINDOW)[:, None] + WINDOW
    si = np.arange(2 * WINDOW)[None, :]
    band_ok = (qi - si >= 0) & (qi - si < WINDOW)
    band = jnp.asarray(np.where(band_ok, 0.0, -np.inf), F32)
    band0 = jnp.asarray(np.where(band_ok & (si >= WINDOW), 0.0, -np.inf), F32)
    qkv_w = w_qkv.shape[1]

    row_spec = pl.BlockSpec((1, tq, d), lambda i, j, *_: (i, j, 0))
    tab_spec = pl.BlockSpec((1, tq, LANES), lambda i, j, *_: (i, j, 0))
    grid_spec = pltpu.PrefetchScalarGridSpec(
        num_scalar_prefetch=1,
        grid=(b, t // tq),
        in_specs=[
            row_spec,
            _const_spec((1, d)),
            _const_spec((d, qkv_w)),
            _const_spec((1, d)),
            _const_spec((1, kw)),
            tab_spec, tab_spec, tab_spec,
            _const_spec((SWA_GW, SWA_GW)),
            _const_spec((WINDOW, 2 * WINDOW)),
            _const_spec((WINDOW, 2 * WINDOW)),
            _const_spec((d, d)),
        ],
        out_specs=row_spec,
        scratch_shapes=[
            pltpu.VMEM((SWA_KV_HEADS, tq, SWA_GW), BF16),
            pltpu.VMEM((SWA_KV_HEADS, WINDOW + tq, SWA_GW), BF16),
            pltpu.VMEM((SWA_KV_HEADS, SWA_GROUP, WINDOW + tq, SWA_GW), BF16),
            pltpu.VMEM((SWA_KV_HEADS, tq, SWA_GW), BF16),
        ],
    )
    return pl.pallas_call(
        _swa_body,
        out_shape=jax.ShapeDtypeStruct((b, t, d), F32),
        grid_spec=grid_spec,
        compiler_params=_params("parallel", "arbitrary"),
        name="swa",
    )(sinks, x, gain.reshape(1, d), w_qkv.astype(BF16), qgain_t, kgain_t, cos_t, sin_lo, sin_hi,
      gmat, band, band0, w_out.astype(BF16))


def kernel(x, mem, positions, mix_norm, hyb_w_in, gla_w_gate2, gla_gate_bias, gla_out_gain, conv_w,
           hyb_w_out, swa_w_qkv, swa_q_gain, swa_k_gain, swa_sinks, swa_w_out, mem_norm, xa_norm, xa_wq,
           xa_wkv, xa_q_gain, xa_k_gain, xa_wo, ffn_norm, ffn_w_gate_up, ffn_w_down):
    b, t, d = x.shape
    depth = mix_norm.shape[0]
    for layer in range(depth):
        i = layer // 2
        if layer % 2 == 0:
            x = _gla_layer(x, mix_norm[layer], hyb_w_in[i], gla_w_gate2[i], gla_gate_bias[i],
                           gla_out_gain[i], conv_w[i], hyb_w_out[i])
        else:
            x = _swa_layer(x, positions, mix_norm[layer], swa_w_qkv[i], swa_q_gain[i], swa_k_gain[i],
                           swa_sinks[i], swa_w_out[i])
        x = _cross_attention(x, mem, mem_norm[layer], xa_norm[layer], xa_wq[layer], xa_wkv[layer],
                             xa_q_gain[layer], xa_k_gain[layer], xa_wo[layer])
        x = _ffn(x.reshape(b * t, d), ffn_norm[layer], ffn_w_gate_up[layer],
                 ffn_w_down[layer]).reshape(b, t, d)
    return x
```

```python
import functools

import numpy as np
import jax
import jax.numpy as jnp
from jax import lax
from jax.experimental import pallas as pl
from jax.experimental.pallas import tpu as pltpu

F32 = jnp.float32
BF16 = jnp.bfloat16

D_MODEL = 1024
RMS_EPS = 1e-6
GLA_HEADS = 4
GLA_DK = 64
GLA_DV = 128
GLA_RANK = 16
GLA_TAU = 16.0
GLA_CHUNK = 64
GLA_KW = GLA_HEADS * GLA_DK
GLA_VW = GLA_HEADS * GLA_DV
CONV_DIM = 512
CONV_WIDTH = 3
RANK_PAD = 128
SWA_HEADS = 16
SWA_KV_HEADS = 4
SWA_GROUP = SWA_HEADS // SWA_KV_HEADS
SWA_HEAD_DIM = 64
WINDOW = 128
ROT_DIM = 16
ROPE_THETA = 500000.0
SWA_GW = SWA_GROUP * SWA_HEAD_DIM
XA_HEADS = 4
XA_HEAD_DIM = 256
FFN_HIDDEN = 2816
FFN_CHUNK = 256

LANES = 128
SUBLANES = 8
VMEM_LIMIT = 56 * 1024 * 1024

GLA_TM = 256
SWA_TQ = 512
XA_TM = 512
FFN_TM = 512


def _rms(x, gain):
    return x * lax.rsqrt(jnp.mean(x * x, axis=-1, keepdims=True) + RMS_EPS) * gain


def _dot(a, b):
    return jnp.dot(a, b, preferred_element_type=F32)


def _dot_nt(a, b):
    return lax.dot_general(a, b, (((1,), (1,)), ((), ())), preferred_element_type=F32)


def _dot_tn(a, b):
    return lax.dot_general(a, b, (((0,), (0,)), ((), ())), preferred_element_type=F32)


def _split_bf16(x):
    hi = x.astype(BF16)
    lo = (x - hi.astype(F32)).astype(BF16)
    return hi, lo


def _const_spec(shape):
    return pl.BlockSpec(shape, lambda *_: (0,) * len(shape))


def _params(*semantics, flags=None):
    return pltpu.CompilerParams(dimension_semantics=semantics, vmem_limit_bytes=VMEM_LIMIT, flags=flags)


def _ffn_body(x_ref, gain_ref, wg_ref, wu_ref, wd_ref, o_ref, h_ref, acc_ref):
    x = x_ref[...]
    h_ref[...] = _rms(x, gain_ref[...]).astype(BF16)
    acc_ref[...] = x
    for c in range(FFN_HIDDEN // FFN_CHUNK):
        h = h_ref[...]
        g = _dot(h, wg_ref[c])
        u = _dot(h, wu_ref[c])
        a = (g / (1.0 + jnp.exp(-g)) * u).astype(BF16)
        acc_ref[...] += _dot(a, wd_ref[c])
    o_ref[...] = acc_ref[...]


def _ffn(x2, gain, w_gate_up, w_down):
    m, d = x2.shape
    nc = FFN_HIDDEN // FFN_CHUNK
    wg = w_gate_up[:, :FFN_HIDDEN].reshape(d, nc, FFN_CHUNK).transpose(1, 0, 2).astype(BF16)
    wu = w_gate_up[:, FFN_HIDDEN:].reshape(d, nc, FFN_CHUNK).transpose(1, 0, 2).astype(BF16)
    wd = w_down.reshape(nc, FFN_CHUNK, d).astype(BF16)
    tm = min(FFN_TM, m)
    return pl.pallas_call(
        _ffn_body,
        out_shape=jax.ShapeDtypeStruct((m, d), F32),
        grid=(m // tm,),
        in_specs=[
            pl.BlockSpec((tm, d), lambda i: (i, 0)),
            _const_spec((1, d)),
            _const_spec((nc, d, FFN_CHUNK)),
            _const_spec((nc, d, FFN_CHUNK)),
            _const_spec((nc, FFN_CHUNK, d)),
        ],
        out_specs=pl.BlockSpec((tm, d), lambda i: (i, 0)),
        scratch_shapes=[pltpu.VMEM((tm, d), BF16), pltpu.VMEM((tm, d), F32)],
        compiler_params=_params("parallel"),
        name="ffn",
    )(x2, gain.reshape(1, d), wg, wu, wd)


def _xa_kv_body(mem_ref, gain_ref, wkv_ref, kgain_ref, k_ref, v_ref):
    mh = _rms(mem_ref[0], gain_ref[...]).astype(BF16)
    kv = _dot(mh, wkv_ref[...])
    for h in range(XA_HEADS):
        lo = h * XA_HEAD_DIM
        k_ref[0, :, lo:lo + XA_HEAD_DIM] = _rms(kv[:, lo:lo + XA_HEAD_DIM], kgain_ref[...]).astype(BF16)
    v_ref[0] = kv[:, D_MODEL:].astype(BF16)


def _xa_body(x_ref, gain_ref, wq_ref, qgain_ref, k_ref, v_ref, wo_ref, o_ref, att_ref):
    x = x_ref[0]
    h = _rms(x, gain_ref[...]).astype(BF16)
    q = _dot(h, wq_ref[...])
    scale = XA_HEAD_DIM ** -0.5
    for hd in range(XA_HEADS):
        lo = hd * XA_HEAD_DIM
        qh = (_rms(q[:, lo:lo + XA_HEAD_DIM], qgain_ref[...]) * scale).astype(BF16)
        s = _dot_nt(qh, k_ref[0, :, lo:lo + XA_HEAD_DIM])
        p = jnp.exp(s - jnp.max(s, axis=-1, keepdims=True))
        p = p / jnp.sum(p, axis=-1, keepdims=True)
        att_ref[:, lo:lo + XA_HEAD_DIM] = _dot(p.astype(BF16), v_ref[0, :, lo:lo + XA_HEAD_DIM]).astype(BF16)
    o_ref[0] = x + _dot(att_ref[...], wo_ref[...])


def _cross_attention(x, mem, mem_gain, xa_gain, wq, wkv, q_gain, k_gain, wo):
    b, t, d = x.shape
    m = mem.shape[1]
    k, v = pl.pallas_call(
        _xa_kv_body,
        out_shape=(jax.ShapeDtypeStruct((b, m, d), BF16), jax.ShapeDtypeStruct((b, m, d), BF16)),
        grid=(b,),
        in_specs=[
            pl.BlockSpec((1, m, d), lambda i: (i, 0, 0)),
            _const_spec((1, d)),
            _const_spec((d, 2 * d)),
            _const_spec((1, XA_HEAD_DIM)),
        ],
        out_specs=(pl.BlockSpec((1, m, d), lambda i: (i, 0, 0)), pl.BlockSpec((1, m, d), lambda i: (i, 0, 0))),
        compiler_params=_params("parallel"),
        name="xa_kv",
    )(mem, mem_gain.reshape(1, d), wkv.astype(BF16), k_gain.reshape(1, XA_HEAD_DIM))
    tm = min(XA_TM, t)
    return pl.pallas_call(
        _xa_body,
        out_shape=jax.ShapeDtypeStruct((b, t, d), F32),
        grid=(b, t // tm),
        in_specs=[
            pl.BlockSpec((1, tm, d), lambda i, j: (i, j, 0)),
            _const_spec((1, d)),
            _const_spec((d, d)),
            _const_spec((1, XA_HEAD_DIM)),
            pl.BlockSpec((1, m, d), lambda i, j: (i, 0, 0)),
            pl.BlockSpec((1, m, d), lambda i, j: (i, 0, 0)),
            _const_spec((d, d)),
        ],
        out_specs=pl.BlockSpec((1, tm, d), lambda i, j: (i, j, 0)),
        scratch_shapes=[pltpu.VMEM((tm, d), BF16)],
        compiler_params=_params("parallel", "parallel"),
        name="xa",
    )(x, xa_gain.reshape(1, d), wq.astype(BF16), q_gain.reshape(1, XA_HEAD_DIM), k, v, wo.astype(BF16))


_GLA_Q, _GLA_K, _GLA_V, _GLA_GO, _GLA_CB, _GLA_CC, _GLA_CI, _GLA_GL = (
    0, 256, 512, 1024, 1536, 2048, 2560, 3072)
GLA_PROJ_W = _GLA_GL + RANK_PAD


def _gla_body(x_ref, gain_ref, win_ref, wg2_ref, gbias_ref, ogain_ref, convw_ref, wout_ref,
              tri_ref, ones_ref, hmask_ref, smask_ref, o_ref,
              st_ref, ubuf_ref, utail_ref, cat_ref):
    tm = x_ref.shape[1]
    first = pl.program_id(1) == 0

    @pl.when(first)
    def _():
        st_ref[...] = jnp.zeros_like(st_ref)
        utail_ref[...] = jnp.zeros((SUBLANES, CONV_DIM), F32)

    x = x_ref[0]
    h = _rms(x, gain_ref[...]).astype(BF16)
    proj = _dot(h, win_ref[...])

    logits = _dot(proj[:, _GLA_GL:_GLA_GL + RANK_PAD].astype(BF16), wg2_ref[...]) + gbias_ref[...]
    log_a = (jnp.minimum(logits, 0.0) - jnp.log(1.0 + jnp.exp(-jnp.abs(logits)))) * (1.0 / GLA_TAU)
    la_hi, la_lo = _split_bf16(log_a)
    b = _dot(tri_ref[...], la_hi) + _dot(tri_ref[...], la_lo)
    b_last = _dot(ones_ref[...], la_hi) + _dot(ones_ref[...], la_lo)

    q = proj[:, _GLA_Q:_GLA_Q + GLA_KW] * (GLA_DK ** -0.5)
    k = proj[:, _GLA_K:_GLA_K + GLA_KW]
    v = proj[:, _GLA_V:_GLA_V + GLA_VW].astype(BF16)
    q_in = (q * jnp.exp(b)).astype(BF16)
    k_in = (k * jnp.exp(-b)).astype(BF16)
    k_out = (k * jnp.exp(b_last - b)).astype(BF16)
    decay = jnp.exp(b_last)

    n_chunks = tm // GLA_CHUNK
    o_inter = []
    for c in range(n_chunks):
        r0 = c * GLA_CHUNK
        st = st_ref[...]
        o_inter.append(_dot_nt(q_in[r0:r0 + GLA_CHUNK], st.astype(BF16)))
        kv_t = _dot_tn(v[r0:r0 + GLA_CHUNK], k_out[r0:r0 + GLA_CHUNK])
        st_ref[...] = st * decay[r0:r0 + 1, :] + jnp.where(smask_ref[...] > 0.5, kv_t, 0.0)
    o_inter = jnp.concatenate(o_inter, axis=0)

    causal = tri_ref[...] > 0.5
    g_out = proj[:, _GLA_GO:_GLA_GO + GLA_VW]
    for hd in range(GLA_HEADS):
        lo = hd * GLA_DV
        q_h = jnp.where(hmask_ref[hd:hd + 1, :] > 0.5, q_in, jnp.zeros_like(q_in))
        scores = jnp.where(causal, _dot_nt(q_h, k_in), 0.0)
        o_h = _dot(scores.astype(BF16), v[:, lo:lo + GLA_DV]) + o_inter[:, lo:lo + GLA_DV]
        g_h = g_out[:, lo:lo + GLA_DV]
        y_h = _rms(o_h, ogain_ref[...]) * (g_h / (1.0 + jnp.exp(-g_h)))
        cat_ref[:, lo:lo + GLA_DV] = y_h.astype(BF16)

    u = proj[:, _GLA_CC:_GLA_CC + CONV_DIM] * proj[:, _GLA_CI:_GLA_CI + CONV_DIM]
    ubuf_ref[0:SUBLANES, :] = utail_ref[...]
    ubuf_ref[SUBLANES:SUBLANES + tm, :] = u
    y = (convw_ref[0:1, :] * ubuf_ref[SUBLANES - 2:SUBLANES - 2 + tm, :]
         + convw_ref[1:2, :] * ubuf_ref[SUBLANES - 1:SUBLANES - 1 + tm, :]
         + convw_ref[2:3, :] * u)
    cat_ref[:, GLA_VW:GLA_VW + CONV_DIM] = (proj[:, _GLA_CB:_GLA_CB + CONV_DIM] * y).astype(BF16)

    o_ref[0] = x + _dot(cat_ref[...], wout_ref[...])

    @pl.when(pl.program_id(1) + 1 < pl.num_programs(1))
    def _():
        utail_ref[...] = ubuf_ref[tm:tm + SUBLANES, :]


def _gla_layer(x, gain, w_in, w_gate2, gate_bias, out_gain, conv_w, w_out):
    b, t, d = x.shape
    tm = min(GLA_TM, t)
    widths = [GLA_KW, GLA_KW, GLA_VW, GLA_RANK, GLA_VW, CONV_DIM, CONV_DIM, CONV_DIM]
    cuts = [int(c) for c in np.cumsum(widths)[:-1]]
    wq, wk, wv, wgl, wgo, wcb, wcc, wci = jnp.split(w_in, cuts, axis=1)
    wgl = jnp.pad(wgl, ((0, 0), (0, RANK_PAD - GLA_RANK)))
    w_in_r = jnp.concatenate([wq, wk, wv, wgo, wcb, wcc, wci, wgl], axis=1).astype(BF16)
    wg2 = jnp.pad(w_gate2, ((0, RANK_PAD - GLA_RANK), (0, 0))).astype(BF16)

    r = np.arange(tm)
    same_chunk = (r[:, None] // GLA_CHUNK) == (r[None, :] // GLA_CHUNK)
    tri = jnp.asarray(same_chunk & (r[None, :] <= r[:, None]), BF16)
    ones = jnp.asarray(same_chunk, BF16)
    hmask = jnp.asarray(np.arange(GLA_KW)[None, :] // GLA_DK == np.arange(GLA_HEADS)[:, None], F32)
    smask = jnp.asarray(
        (np.arange(GLA_VW)[:, None] // GLA_DV) == (np.arange(GLA_KW)[None, :] // GLA_DK), F32)

    return pl.pallas_call(
        _gla_body,
        out_shape=jax.ShapeDtypeStruct((b, t, d), F32),
        grid=(b, t // tm),
        in_specs=[
            pl.BlockSpec((1, tm, d), lambda i, j: (i, j, 0)),
            _const_spec((1, d)),
            _const_spec((d, GLA_PROJ_W)),
            _const_spec((RANK_PAD, GLA_KW)),
            _const_spec((1, GLA_KW)),
            _const_spec((1, GLA_DV)),
            _const_spec((CONV_WIDTH, CONV_DIM)),
            _const_spec((GLA_VW + CONV_DIM, d)),
            _const_spec((tm, tm)),
            _const_spec((tm, tm)),
            _const_spec((GLA_HEADS, GLA_KW)),
            _const_spec((GLA_VW, GLA_KW)),
        ],
        out_specs=pl.BlockSpec((1, tm, d), lambda i, j: (i, j, 0)),
        scratch_shapes=[
            pltpu.VMEM((GLA_VW, GLA_KW), F32),
            pltpu.VMEM((tm + SUBLANES, CONV_DIM), F32),
            pltpu.VMEM((SUBLANES, CONV_DIM), F32),
            pltpu.VMEM((tm, GLA_VW + CONV_DIM), BF16),
        ],
        compiler_params=_params("parallel", "arbitrary"),
        name="gla_conv",
    )(x, gain.reshape(1, d), w_in_r, wg2, gate_bias.reshape(1, GLA_KW), out_gain.reshape(1, GLA_DV),
      conv_w, w_out.astype(BF16), tri, ones, hmask, smask)


def _norm_rope_heads(z, n_heads, gain, cos_f, sin_f):
    hd, half = SWA_HEAD_DIM, ROT_DIM // 2
    out = []
    for head in range(n_heads):
        zh = z[head * hd:(head + 1) * hd, :]
        ss = jnp.sum(zh * zh, axis=0, keepdims=True)
        zh = zh * lax.rsqrt(ss * (1.0 / hd) + RMS_EPS) * gain
        x1, x2 = zh[0:half], zh[half:ROT_DIM]
        out.append(jnp.concatenate(
            [x1 * cos_f - x2 * sin_f, x2 * cos_f + x1 * sin_f, zh[ROT_DIM:]], axis=0))
    return out


def _swa_body(x_ref, gain_ref, wqkv_t_ref, qgain_ref, kgain_ref, cos_ref, sin_ref,
              band_ref, band0_ref, sink_ref, wout_ref, o_ref,
              qpad_ref, kbuf_ref, vbuf_ref, kprev_ref, vprev_ref, att_ref):
    tq = x_ref.shape[1]
    nb = tq // WINDOW
    n = pl.program_id(1)
    hd = SWA_HEAD_DIM
    qw = SWA_HEADS * hd
    kw = SWA_KV_HEADS * hd

    @pl.when(n == 0)
    def _():
        qpad_ref[...] = jnp.zeros_like(qpad_ref)
        kprev_ref[...] = jnp.zeros((WINDOW, kw), BF16)
        vprev_ref[...] = jnp.zeros((kw, WINDOW), BF16)

    x = x_ref[0]
    h = _rms(x, gain_ref[...]).astype(BF16)
    qkv_t = _dot_nt(wqkv_t_ref[...], h)
    cos_f, sin_f = cos_ref[0], sin_ref[0]

    k_heads = _norm_rope_heads(qkv_t[qw:qw + kw], SWA_KV_HEADS, kgain_ref[...], cos_f, sin_f)
    k_tok = jnp.concatenate(k_heads, axis=0).T.astype(BF16)
    v_t = qkv_t[qw + kw:, :].astype(BF16)
    kbuf_ref[0:WINDOW, :] = kprev_ref[...]
    kbuf_ref[WINDOW:WINDOW + tq, :] = k_tok
    vbuf_ref[:, 0:WINDOW] = vprev_ref[...]
    vbuf_ref[:, WINDOW:WINDOW + tq] = v_t

    q_heads = _norm_rope_heads(qkv_t[:qw], SWA_HEADS, qgain_ref[...], cos_f, sin_f)
    for head, z in enumerate(q_heads):
        g, j = divmod(head, SWA_GROUP)
        z = z.astype(BF16)
        for i in range(nb):
            qpad_ref[g, i, g * hd:(g + 1) * hd, j * WINDOW:(j + 1) * WINDOW] = z[:, i * WINDOW:(i + 1) * WINDOW]

    bias0 = jnp.where(n > 0, band_ref[...], band0_ref[...])

    units = [(g, i) for g in range(SWA_KV_HEADS) for i in range(nb)]

    def scores(unit):
        g, i = unit
        return _dot(kbuf_ref[i * WINDOW:(i + 2) * WINDOW, :], qpad_ref[g, i])

    s_next = scores(units[0])
    for u, (g, i) in enumerate(units):
        r0 = i * WINDOW
        s = s_next + (bias0 if i == 0 else band_ref[...])
        sink = sink_ref[g]
        m = jnp.maximum(jnp.max(s, axis=0, keepdims=True), sink)
        if u + 1 < len(units):
            s_next = scores(units[u + 1])
            m = m + jnp.minimum(jnp.abs(s_next[0:1, :]), 0.0)
        p = jnp.exp(s - m)
        denom = jnp.sum(p, axis=0, keepdims=True) + jnp.exp(sink - m)
        o = _dot(vbuf_ref[g * hd:(g + 1) * hd, r0:r0 + 2 * WINDOW], p.astype(BF16)) * (1.0 / denom)
        for j in range(SWA_GROUP):
            head = g * SWA_GROUP + j
            att_ref[head * hd:(head + 1) * hd, r0:r0 + WINDOW] = o[:, j * WINDOW:(j + 1) * WINDOW].astype(BF16)

    o_ref[0] = x + _dot_tn(att_ref[...], wout_ref[...])

    @pl.when(n + 1 < pl.num_programs(1))
    def _():
        kprev_ref[...] = kbuf_ref[tq:tq + WINDOW, :]
        vprev_ref[...] = vbuf_ref[:, tq:tq + WINDOW]


def _swa_layer(x, positions, gain, w_qkv, q_gain, k_gain, sinks, w_out):
    b, t, d = x.shape
    tq = min(SWA_TQ, t)
    nb = tq // WINDOW
    hd = SWA_HEAD_DIM
    qw = SWA_HEADS * hd
    kw = SWA_KV_HEADS * hd
    inv_freq = ROPE_THETA ** (-jnp.arange(0, ROT_DIM, 2, dtype=F32) / ROT_DIM)
    ang = positions.astype(F32)[:, None, :] * inv_freq[None, :, None]
    cos_f, sin_f = jnp.cos(ang), jnp.sin(ang)
    qgain_f = jnp.broadcast_to((q_gain * (hd ** -0.5))[:, None], (hd, tq))
    kgain_f = jnp.broadcast_to(k_gain[:, None], (hd, tq))
    rows = SWA_GROUP * WINDOW
    si = np.arange(2 * WINDOW)[:, None]
    qi = np.arange(WINDOW)[None, :] + WINDOW
    band_ok = (qi - si >= 0) & (qi - si < WINDOW)
    band = jnp.asarray(np.tile(np.where(band_ok, 0.0, -np.inf), (1, SWA_GROUP)), F32)
    band0 = jnp.asarray(np.tile(np.where(band_ok & (si >= WINDOW), 0.0, -np.inf), (1, SWA_GROUP)), F32)
    sink_cols = jnp.repeat(sinks.astype(F32).reshape(SWA_KV_HEADS, 1, SWA_GROUP), WINDOW, axis=2)

    row_spec = pl.BlockSpec((1, tq, d), lambda i, j: (i, j, 0))
    tab_spec = pl.BlockSpec((1, ROT_DIM // 2, tq), lambda i, j: (i, 0, j))
    return pl.pallas_call(
        _swa_body,
        out_shape=jax.ShapeDtypeStruct((b, t, d), F32),
        grid=(b, t // tq),
        in_specs=[
            row_spec,
            _const_spec((1, d)),
            _const_spec((qw + 2 * kw, d)),
            _const_spec((hd, tq)),
            _const_spec((hd, tq)),
            tab_spec, tab_spec,
            _const_spec((2 * WINDOW, rows)),
            _const_spec((2 * WINDOW, rows)),
            _const_spec((SWA_KV_HEADS, 1, rows)),
            _const_spec((d, d)),
        ],
        out_specs=row_spec,
        scratch_shapes=[
            pltpu.VMEM((SWA_KV_HEADS, nb, kw, rows), BF16),
            pltpu.VMEM((WINDOW + tq, kw), BF16),
            pltpu.VMEM((kw, WINDOW + tq), BF16),
            pltpu.VMEM((WINDOW, kw), BF16),
            pltpu.VMEM((kw, WINDOW), BF16),
            pltpu.VMEM((qw, tq), BF16),
        ],
        compiler_params=_params("parallel", "arbitrary"),
        name="swa",
    )(x, gain.reshape(1, d), w_qkv.T.astype(BF16), qgain_f, kgain_f, cos_f, sin_f,
      band, band0, sink_cols, w_out.astype(BF16))


def kernel(x, mem, positions, mix_norm, hyb_w_in, gla_w_gate2, gla_gate_bias, gla_out_gain, conv_w,
           hyb_w_out, swa_w_qkv, swa_q_gain, swa_k_gain, swa_sinks, swa_w_out, mem_norm, xa_norm, xa_wq,
           xa_wkv, xa_q_gain, xa_k_gain, xa_wo, ffn_norm, ffn_w_gate_up, ffn_w_down):
    b, t, d = x.shape
    depth = mix_norm.shape[0]
    for layer in range(depth):
        i = layer // 2
        if layer % 2 == 0:
            x = _gla_layer(x, mix_norm[layer], hyb_w_in[i], gla_w_gate2[i], gla_gate_bias[i],
                           gla_out_gain[i], conv_w[i], hyb_w_out[i])
        else:
            x = _swa_layer(x, positions, mix_norm[layer], swa_w_qkv[i], swa_q_gain[i], swa_k_gain[i],
                           swa_sinks[i], swa_w_out[i])
        x = _cross_attention(x, mem, mem_norm[layer], xa_norm[layer], xa_wq[layer], xa_wkv[layer],
                             xa_q_gain[layer], xa_k_gain[layer], xa_wo[layer])
        x = _ffn(x.reshape(b * t, d), ffn_norm[layer], ffn_w_gate_up[layer],
                 ffn_w_down[layer]).reshape(b, t, d)
    return x
```

```python
import functools

import numpy as np
import jax
import jax.numpy as jnp
from jax import lax
from jax.experimental import pallas as pl
from jax.experimental.pallas import tpu as pltpu

F32 = jnp.float32
BF16 = jnp.bfloat16

D_MODEL = 1024
RMS_EPS = 1e-6
GLA_HEADS = 4
GLA_DK = 64
GLA_DV = 128
GLA_RANK = 16
GLA_TAU = 16.0
GLA_CHUNK = 64
GLA_KW = GLA_HEADS * GLA_DK
GLA_VW = GLA_HEADS * GLA_DV
CONV_DIM = 512
CONV_WIDTH = 3
RANK_PAD = 128
SWA_HEADS = 16
SWA_KV_HEADS = 4
SWA_GROUP = SWA_HEADS // SWA_KV_HEADS
SWA_HEAD_DIM = 64
WINDOW = 128
ROT_DIM = 16
ROPE_THETA = 500000.0
SWA_GW = SWA_GROUP * SWA_HEAD_DIM
XA_HEADS = 4
XA_HEAD_DIM = 256
FFN_HIDDEN = 2816
FFN_CHUNK = 256

LANES = 128
SUBLANES = 8
VMEM_LIMIT = 56 * 1024 * 1024

GLA_TM = 256
GLA_ROWS = 4
SWA_TQ = 512
SWA_ROWS = 2
SWA_LOOKAHEAD = 4
XA_TM = 512
FFN_TM = 512


def _rms(x, gain):
    return x * lax.rsqrt(jnp.mean(x * x, axis=-1, keepdims=True) + RMS_EPS) * gain


def _dot(a, b):
    return jnp.dot(a, b, preferred_element_type=F32)


def _dot_nt(a, b):
    return lax.dot_general(a, b, (((1,), (1,)), ((), ())), preferred_element_type=F32)


def _dot_tn(a, b):
    return lax.dot_general(a, b, (((0,), (0,)), ((), ())), preferred_element_type=F32)


def _split_bf16(x):
    hi = x.astype(BF16)
    lo = (x - hi.astype(F32)).astype(BF16)
    return hi, lo


def _const_spec(shape):
    return pl.BlockSpec(shape, lambda *_: (0,) * len(shape))


def _params(*semantics, flags=None):
    return pltpu.CompilerParams(dimension_semantics=semantics, vmem_limit_bytes=VMEM_LIMIT, flags=flags)


def _ffn_body(x_ref, gain_ref, wg_ref, wu_ref, wd_ref, o_ref, h_ref, acc_ref):
    x = x_ref[...]
    h_ref[...] = _rms(x, gain_ref[...]).astype(BF16)
    acc_ref[...] = x
    for c in range(FFN_HIDDEN // FFN_CHUNK):
        h = h_ref[...]
        g = _dot(h, wg_ref[c])
        u = _dot(h, wu_ref[c])
        a = (g / (1.0 + jnp.exp(-g)) * u).astype(BF16)
        acc_ref[...] += _dot(a, wd_ref[c])
    o_ref[...] = acc_ref[...]


def _ffn(x2, gain, w_gate_up, w_down):
    m, d = x2.shape
    nc = FFN_HIDDEN // FFN_CHUNK
    wg = w_gate_up[:, :FFN_HIDDEN].reshape(d, nc, FFN_CHUNK).transpose(1, 0, 2).astype(BF16)
    wu = w_gate_up[:, FFN_HIDDEN:].reshape(d, nc, FFN_CHUNK).transpose(1, 0, 2).astype(BF16)
    wd = w_down.reshape(nc, FFN_CHUNK, d).astype(BF16)
    tm = min(FFN_TM, m)
    return pl.pallas_call(
        _ffn_body,
        out_shape=jax.ShapeDtypeStruct((m, d), F32),
        grid=(m // tm,),
        in_specs=[
            pl.BlockSpec((tm, d), lambda i: (i, 0)),
            _const_spec((1, d)),
            _const_spec((nc, d, FFN_CHUNK)),
            _const_spec((nc, d, FFN_CHUNK)),
            _const_spec((nc, FFN_CHUNK, d)),
        ],
        out_specs=pl.BlockSpec((tm, d), lambda i: (i, 0)),
        scratch_shapes=[pltpu.VMEM((tm, d), BF16), pltpu.VMEM((tm, d), F32)],
        compiler_params=_params("parallel"),
        name="ffn",
    )(x2, gain.reshape(1, d), wg, wu, wd)


def _xa_kv_body(mem_ref, gain_ref, wkv_ref, kgain_ref, k_ref, v_ref):
    mh = _rms(mem_ref[0], gain_ref[...]).astype(BF16)
    kv = _dot(mh, wkv_ref[...])
    for h in range(XA_HEADS):
        lo = h * XA_HEAD_DIM
        k_ref[0, :, lo:lo + XA_HEAD_DIM] = _rms(kv[:, lo:lo + XA_HEAD_DIM], kgain_ref[...]).astype(BF16)
    v_ref[0] = kv[:, D_MODEL:].astype(BF16)


def _xa_body(x_ref, gain_ref, wq_ref, qgain_ref, k_ref, v_ref, wo_ref, o_ref, att_ref):
    x = x_ref[0]
    h = _rms(x, gain_ref[...]).astype(BF16)
    q = _dot(h, wq_ref[...])
    scale = XA_HEAD_DIM ** -0.5
    for hd in range(XA_HEADS):
        lo = hd * XA_HEAD_DIM
        qh = (_rms(q[:, lo:lo + XA_HEAD_DIM], qgain_ref[...]) * scale).astype(BF16)
        s = _dot_nt(qh, k_ref[0, :, lo:lo + XA_HEAD_DIM])
        p = jnp.exp(s - jnp.max(s, axis=-1, keepdims=True))
        p = p / jnp.sum(p, axis=-1, keepdims=True)
        att_ref[:, lo:lo + XA_HEAD_DIM] = _dot(p.astype(BF16), v_ref[0, :, lo:lo + XA_HEAD_DIM]).astype(BF16)
    o_ref[0] = x + _dot(att_ref[...], wo_ref[...])


def _cross_attention(x, mem, mem_gain, xa_gain, wq, wkv, q_gain, k_gain, wo):
    b, t, d = x.shape
    m = mem.shape[1]
    k, v = pl.pallas_call(
        _xa_kv_body,
        out_shape=(jax.ShapeDtypeStruct((b, m, d), BF16), jax.ShapeDtypeStruct((b, m, d), BF16)),
        grid=(b,),
        in_specs=[
            pl.BlockSpec((1, m, d), lambda i: (i, 0, 0)),
            _const_spec((1, d)),
            _const_spec((d, 2 * d)),
            _const_spec((1, XA_HEAD_DIM)),
        ],
        out_specs=(pl.BlockSpec((1, m, d), lambda i: (i, 0, 0)), pl.BlockSpec((1, m, d), lambda i: (i, 0, 0))),
        compiler_params=_params("parallel"),
        name="xa_kv",
    )(mem, mem_gain.reshape(1, d), wkv.astype(BF16), k_gain.reshape(1, XA_HEAD_DIM))
    tm = min(XA_TM, t)
    return pl.pallas_call(
        _xa_body,
        out_shape=jax.ShapeDtypeStruct((b, t, d), F32),
        grid=(b, t // tm),
        in_specs=[
            pl.BlockSpec((1, tm, d), lambda i, j: (i, j, 0)),
            _const_spec((1, d)),
            _const_spec((d, d)),
            _const_spec((1, XA_HEAD_DIM)),
            pl.BlockSpec((1, m, d), lambda i, j: (i, 0, 0)),
            pl.BlockSpec((1, m, d), lambda i, j: (i, 0, 0)),
            _const_spec((d, d)),
        ],
        out_specs=pl.BlockSpec((1, tm, d), lambda i, j: (i, j, 0)),
        scratch_shapes=[pltpu.VMEM((tm, d), BF16)],
        compiler_params=_params("parallel", "parallel"),
        name="xa",
    )(x, xa_gain.reshape(1, d), wq.astype(BF16), q_gain.reshape(1, XA_HEAD_DIM), k, v, wo.astype(BF16))


_GLA_Q, _GLA_K, _GLA_V, _GLA_GO, _GLA_CB, _GLA_CC, _GLA_CI, _GLA_GL = (
    0, 256, 512, 1024, 1536, 2048, 2560, 3072)
GLA_PROJ_W = _GLA_GL + RANK_PAD


def _gla_body(x_ref, gain_ref, win_ref, wg2_ref, gbias_ref, ogain_ref, convw_ref, wout_ref,
              tri_ref, ones_ref, hmask_ref, smask_ref, o_ref,
              st_ref, ubuf_ref, utail_ref, cat_ref):
    rows, tm = x_ref.shape[0], x_ref.shape[1]
    seqs = range(rows)

    @pl.when(pl.program_id(1) == 0)
    def _():
        st_ref[...] = jnp.zeros_like(st_ref)
        utail_ref[...] = jnp.zeros_like(utail_ref)

    xs, projs, bs, b_lasts = [], [], [], []
    q_in, k_in, k_out, decay, v = [], [], [], [], []

    def project(r):
        xs.append(x_ref[r])
        projs.append(_dot(_rms(xs[r], gain_ref[...]).astype(BF16), win_ref[...]))

    def gates(r):
        logits = _dot(projs[r][:, _GLA_GL:_GLA_GL + RANK_PAD].astype(BF16), wg2_ref[...]) + gbias_ref[...]
        log_a = (jnp.minimum(logits, 0.0) - jnp.log(1.0 + jnp.exp(-jnp.abs(logits)))) * (1.0 / GLA_TAU)
        la_hi, la_lo = _split_bf16(log_a)
        bs.append(_dot(tri_ref[...], la_hi) + _dot(tri_ref[...], la_lo))
        b_lasts.append(_dot(ones_ref[...], la_hi) + _dot(ones_ref[...], la_lo))

    def vector_stage(r):
        proj, b, b_last = projs[r], bs[r], b_lasts[r]
        q = proj[:, _GLA_Q:_GLA_Q + GLA_KW] * (GLA_DK ** -0.5)
        k = proj[:, _GLA_K:_GLA_K + GLA_KW]
        v.append(proj[:, _GLA_V:_GLA_V + GLA_VW].astype(BF16))
        q_in.append((q * jnp.exp(b)).astype(BF16))
        k_in.append((k * jnp.exp(-b)).astype(BF16))
        k_out.append((k * jnp.exp(b_last - b)).astype(BF16))
        decay.append(jnp.exp(b_last))
        u = proj[:, _GLA_CC:_GLA_CC + CONV_DIM] * proj[:, _GLA_CI:_GLA_CI + CONV_DIM]
        ubuf_ref[r, 0:SUBLANES, :] = utail_ref[r]
        ubuf_ref[r, SUBLANES:SUBLANES + tm, :] = u
        y = (convw_ref[0:1, :] * ubuf_ref[r, SUBLANES - 2:SUBLANES - 2 + tm, :]
             + convw_ref[1:2, :] * ubuf_ref[r, SUBLANES - 1:SUBLANES - 1 + tm, :]
             + convw_ref[2:3, :] * u)
        cat_ref[r, :, GLA_VW:GLA_VW + CONV_DIM] = (proj[:, _GLA_CB:_GLA_CB + CONV_DIM] * y).astype(BF16)

    project(0)
    gates(0)
    for r in range(1, rows):
        project(r)
        vector_stage(r - 1)
        gates(r)
    vector_stage(rows - 1)

    o_inter = [[] for _ in seqs]
    for c in range(tm // GLA_CHUNK):
        r0 = c * GLA_CHUNK
        for r in seqs:
            st = st_ref[r]
            o_inter[r].append(_dot_nt(q_in[r][r0:r0 + GLA_CHUNK], st.astype(BF16)))
            kv_t = _dot_tn(v[r][r0:r0 + GLA_CHUNK], k_out[r][r0:r0 + GLA_CHUNK])
            st_ref[r] = st * decay[r][r0:r0 + 1, :] + jnp.where(smask_ref[...] > 0.5, kv_t, 0.0)
    o_inter = [jnp.concatenate(o, axis=0) for o in o_inter]

    causal = tri_ref[...] > 0.5
    for hd in range(GLA_HEADS):
        lo = hd * GLA_DV
        for r in seqs:
            q_h = jnp.where(hmask_ref[hd:hd + 1, :] > 0.5, q_in[r], jnp.zeros_like(q_in[r]))
            scores = jnp.where(causal, _dot_nt(q_h, k_in[r]), 0.0)
            o_h = _dot(scores.astype(BF16), v[r][:, lo:lo + GLA_DV]) + o_inter[r][:, lo:lo + GLA_DV]
            g_h = projs[r][:, _GLA_GO + lo:_GLA_GO + lo + GLA_DV]
            y_h = _rms(o_h, ogain_ref[...]) * (g_h / (1.0 + jnp.exp(-g_h)))
            cat_ref[r, :, lo:lo + GLA_DV] = y_h.astype(BF16)

    for r in seqs:
        o_ref[r] = xs[r] + _dot(cat_ref[r], wout_ref[...])

    @pl.when(pl.program_id(1) + 1 < pl.num_programs(1))
    def _():
        for r in seqs:
            utail_ref[r] = ubuf_ref[r, tm:tm + SUBLANES, :]


def _gla_layer(x, gain, w_in, w_gate2, gate_bias, out_gain, conv_w, w_out):
    b, t, d = x.shape
    tm = min(GLA_TM, t)
    rows = GLA_ROWS
    widths = [GLA_KW, GLA_KW, GLA_VW, GLA_RANK, GLA_VW, CONV_DIM, CONV_DIM, CONV_DIM]
    cuts = [int(c) for c in np.cumsum(widths)[:-1]]
    wq, wk, wv, wgl, wgo, wcb, wcc, wci = jnp.split(w_in, cuts, axis=1)
    wgl = jnp.pad(wgl, ((0, 0), (0, RANK_PAD - GLA_RANK)))
    w_in_r = jnp.concatenate([wq, wk, wv, wgo, wcb, wcc, wci, wgl], axis=1).astype(BF16)
    wg2 = jnp.pad(w_gate2, ((0, RANK_PAD - GLA_RANK), (0, 0))).astype(BF16)

    r = np.arange(tm)
    same_chunk = (r[:, None] // GLA_CHUNK) == (r[None, :] // GLA_CHUNK)
    tri = jnp.asarray(same_chunk & (r[None, :] <= r[:, None]), BF16)
    ones = jnp.asarray(same_chunk, BF16)
    hmask = jnp.asarray(np.arange(GLA_KW)[None, :] // GLA_DK == np.arange(GLA_HEADS)[:, None], F32)
    smask = jnp.asarray(
        (np.arange(GLA_VW)[:, None] // GLA_DV) == (np.arange(GLA_KW)[None, :] // GLA_DK), F32)

    return pl.pallas_call(
        _gla_body,
        out_shape=jax.ShapeDtypeStruct((b, t, d), F32),
        grid=(b // rows, t // tm),
        in_specs=[
            pl.BlockSpec((rows, tm, d), lambda i, j: (i, j, 0)),
            _const_spec((1, d)),
            _const_spec((d, GLA_PROJ_W)),
            _const_spec((RANK_PAD, GLA_KW)),
            _const_spec((1, GLA_KW)),
            _const_spec((1, GLA_DV)),
            _const_spec((CONV_WIDTH, CONV_DIM)),
            _const_spec((GLA_VW + CONV_DIM, d)),
            _const_spec((tm, tm)),
            _const_spec((tm, tm)),
            _const_spec((GLA_HEADS, GLA_KW)),
            _const_spec((GLA_VW, GLA_KW)),
        ],
        out_specs=pl.BlockSpec((rows, tm, d), lambda i, j: (i, j, 0)),
        scratch_shapes=[
            pltpu.VMEM((rows, GLA_VW, GLA_KW), F32),
            pltpu.VMEM((rows, tm + SUBLANES, CONV_DIM), F32),
            pltpu.VMEM((rows, SUBLANES, CONV_DIM), F32),
            pltpu.VMEM((rows, tm, GLA_VW + CONV_DIM), BF16),
        ],
        compiler_params=_params("parallel", "arbitrary"),
        name="gla_conv",
    )(x, gain.reshape(1, d), w_in_r, wg2, gate_bias.reshape(1, GLA_KW), out_gain.reshape(1, GLA_DV),
      conv_w, w_out.astype(BF16), tri, ones, hmask, smask)


def _norm_rope_heads(z, n_heads, gain, cos_f, sin_f):
    hd, half = SWA_HEAD_DIM, ROT_DIM // 2
    out = []
    for head in range(n_heads):
        zh = z[head * hd:(head + 1) * hd, :]
        ss = jnp.sum(zh * zh, axis=0, keepdims=True)
        zh = zh * lax.rsqrt(ss * (1.0 / hd) + RMS_EPS) * gain
        x1, x2 = zh[0:half], zh[half:ROT_DIM]
        out.append(jnp.concatenate(
            [x1 * cos_f - x2 * sin_f, x2 * cos_f + x1 * sin_f, zh[ROT_DIM:]], axis=0))
    return out


def _swa_body(x_ref, gain_ref, wqkv_t_ref, qgain_ref, kgain_ref, cos_ref, sin_ref,
              band_ref, band0_ref, sink_ref, wout_ref, o_ref,
              qpad_ref, kbuf_ref, vbuf_ref, kprev_ref, vprev_ref, att_ref):
    n_seq, tq = x_ref.shape[0], x_ref.shape[1]
    seqs = range(n_seq)
    nb = tq // WINDOW
    n = pl.program_id(1)
    hd = SWA_HEAD_DIM
    qw = SWA_HEADS * hd
    kw = SWA_KV_HEADS * hd

    @pl.when(n == 0)
    def _():
        qpad_ref[...] = jnp.zeros_like(qpad_ref)
        kprev_ref[...] = jnp.zeros_like(kprev_ref)
        vprev_ref[...] = jnp.zeros_like(vprev_ref)

    xs = [x_ref[r] for r in seqs]
    qkv_t = [_dot_nt(wqkv_t_ref[...], _rms(x, gain_ref[...]).astype(BF16)) for x in xs]

    for r in seqs:
        cos_f, sin_f = cos_ref[r], sin_ref[r]
        k_heads = _norm_rope_heads(qkv_t[r][qw:qw + kw], SWA_KV_HEADS, kgain_ref[...], cos_f, sin_f)
        kbuf_ref[r, 0:WINDOW, :] = kprev_ref[r]
        kbuf_ref[r, WINDOW:WINDOW + tq, :] = jnp.concatenate(k_heads, axis=0).T.astype(BF16)
        vbuf_ref[r, :, 0:WINDOW] = vprev_ref[r]
        vbuf_ref[r, :, WINDOW:WINDOW + tq] = qkv_t[r][qw + kw:, :].astype(BF16)
        q_heads = _norm_rope_heads(qkv_t[r][:qw], SWA_HEADS, qgain_ref[...], cos_f, sin_f)
        for head, z in enumerate(q_heads):
            g, j = divmod(head, SWA_GROUP)
            z = z.astype(BF16)
            for i in range(nb):
                qpad_ref[r, g, i, g * hd:(g + 1) * hd, j * WINDOW:(j + 1) * WINDOW] = (
                    z[:, i * WINDOW:(i + 1) * WINDOW])

    bias0 = jnp.where(n > 0, band_ref[...], band0_ref[...])

    units = [(r, g, i) for g in range(SWA_KV_HEADS) for i in range(nb) for r in seqs]

    def scores(unit):
        r, g, i = unit
        return _dot(kbuf_ref[r, i * WINDOW:(i + 2) * WINDOW, :], qpad_ref[r, g, i])

    ones_rows = jnp.ones((2 * SUBLANES, 2 * WINDOW), BF16)
    pending = [scores(unit) for unit in units[:SWA_LOOKAHEAD]]
    for u, (r, g, i) in enumerate(units):
        r0 = i * WINDOW
        s = pending.pop(0) + (bias0 if i == 0 else band_ref[...])
        sink = sink_ref[g]
        m = jnp.maximum(jnp.max(s, axis=0, keepdims=True), sink)
        if u + SWA_LOOKAHEAD < len(units):
            pending.append(scores(units[u + SWA_LOOKAHEAD]))
            m = m + jnp.minimum(jnp.abs(pending[-1][0:1, :]), 0.0)
        p = jnp.exp2(s - m).astype(BF16)
        v_aug = jnp.concatenate([vbuf_ref[r, g * hd:(g + 1) * hd, r0:r0 + 2 * WINDOW], ones_rows], axis=0)
        o_aug = _dot(v_aug, p)
        denom = o_aug[hd:hd + 1, :] + jnp.exp2(sink - m)
        o = o_aug[0:hd, :] * (1.0 / denom)
        for j in range(SWA_GROUP):
            head = g * SWA_GROUP + j
            att_ref[r, head * hd:(head + 1) * hd, r0:r0 + WINDOW] = (
                o[:, j * WINDOW:(j + 1) * WINDOW].astype(BF16))

    for r in seqs:
        o_ref[r] = xs[r] + _dot_tn(att_ref[r], wout_ref[...])

    @pl.when(n + 1 < pl.num_programs(1))
    def _():
        for r in seqs:
            kprev_ref[r] = kbuf_ref[r, tq:tq + WINDOW, :]
            vprev_ref[r] = vbuf_ref[r, :, tq:tq + WINDOW]


def _swa_layer(x, positions, gain, w_qkv, q_gain, k_gain, sinks, w_out):
    b, t, d = x.shape
    tq = min(SWA_TQ, t)
    nb = tq // WINDOW
    hd = SWA_HEAD_DIM
    qw = SWA_HEADS * hd
    kw = SWA_KV_HEADS * hd
    inv_freq = ROPE_THETA ** (-jnp.arange(0, ROT_DIM, 2, dtype=F32) / ROT_DIM)
    ang = positions.astype(F32)[:, None, :] * inv_freq[None, :, None]
    cos_f, sin_f = jnp.cos(ang), jnp.sin(ang)
    log2e = float(np.log2(np.e))
    qgain_f = jnp.broadcast_to((q_gain * (hd ** -0.5 * log2e))[:, None], (hd, tq))
    kgain_f = jnp.broadcast_to(k_gain[:, None], (hd, tq))
    rows = SWA_GROUP * WINDOW
    si = np.arange(2 * WINDOW)[:, None]
    qi = np.arange(WINDOW)[None, :] + WINDOW
    band_ok = (qi - si >= 0) & (qi - si < WINDOW)
    band = jnp.asarray(np.tile(np.where(band_ok, 0.0, -np.inf), (1, SWA_GROUP)), F32)
    band0 = jnp.asarray(np.tile(np.where(band_ok & (si >= WINDOW), 0.0, -np.inf), (1, SWA_GROUP)), F32)
    sink_cols = jnp.repeat((sinks.astype(F32) * log2e).reshape(SWA_KV_HEADS, 1, SWA_GROUP), WINDOW, axis=2)

    n_seq = SWA_ROWS
    row_spec = pl.BlockSpec((n_seq, tq, d), lambda i, j: (i, j, 0))
    tab_spec = pl.BlockSpec((n_seq, ROT_DIM // 2, tq), lambda i, j: (i, 0, j))
    return pl.pallas_call(
        _swa_body,
        out_shape=jax.ShapeDtypeStruct((b, t, d), F32),
        grid=(b // n_seq, t // tq),
        in_specs=[
            row_spec,
            _const_spec((1, d)),
            _const_spec((qw + 2 * kw, d)),
            _const_spec((hd, tq)),
            _const_spec((hd, tq)),
            tab_spec, tab_spec,
            _const_spec((2 * WINDOW, rows)),
            _const_spec((2 * WINDOW, rows)),
            _const_spec((SWA_KV_HEADS, 1, rows)),
            _const_spec((d, d)),
        ],
        out_specs=row_spec,
        scratch_shapes=[
            pltpu.VMEM((n_seq, SWA_KV_HEADS, nb, kw, rows), BF16),
            pltpu.VMEM((n_seq, WINDOW + tq, kw), BF16),
            pltpu.VMEM((n_seq, kw, WINDOW + tq), BF16),
            pltpu.VMEM((n_seq, WINDOW, kw), BF16),
            pltpu.VMEM((n_seq, kw, WINDOW), BF16),
            pltpu.VMEM((n_seq, qw, tq), BF16),
        ],
        compiler_params=_params("parallel", "arbitrary"),
        name="swa",
    )(x, gain.reshape(1, d), w_qkv.T.astype(BF16), qgain_f, kgain_f, cos_f, sin_f,
      band, band0, sink_cols, w_out.astype(BF16))


def kernel(x, mem, positions, mix_norm, hyb_w_in, gla_w_gate2, gla_gate_bias, gla_out_gain, conv_w,
           hyb_w_out, swa_w_qkv, swa_q_gain, swa_k_gain, swa_sinks, swa_w_out, mem_norm, xa_norm, xa_wq,
           xa_wkv, xa_q_gain, xa_k_gain, xa_wo, ffn_norm, ffn_w_gate_up, ffn_w_down):
    b, t, d = x.shape
    depth = mix_norm.shape[0]
    for layer in range(depth):
        i = layer // 2
        if layer % 2 == 0:
            x = _gla_layer(x, mix_norm[layer], hyb_w_in[i], gla_w_gate2[i], gla_gate_bias[i],
                           gla_out_gain[i], conv_w[i], hyb_w_out[i])
        else:
            x = _swa_layer(x, positions, mix_norm[layer], swa_w_qkv[i], swa_q_gain[i], swa_k_gain[i],
                           swa_sinks[i], swa_w_out[i])
        x = _cross_attention(x, mem, mem_norm[layer], xa_norm[layer], xa_wq[layer], xa_wkv[layer],
                             xa_q_gain[layer], xa_k_gain[layer], xa_wo[layer])
        x = _ffn(x.reshape(b * t, d), ffn_norm[layer], ffn_w_gate_up[layer],
                 ffn_w_down[layer]).reshape(b, t, d)
    return x
```

```python
import functools

import numpy as np
import jax
import jax.numpy as jnp
from jax import lax
from jax.experimental import pallas as pl
from jax.experimental.pallas import tpu as pltpu

F32 = jnp.float32
BF16 = jnp.bfloat16

D_MODEL = 1024
RMS_EPS = 1e-6
GLA_HEADS = 4
GLA_DK = 64
GLA_DV = 128
GLA_RANK = 16
GLA_TAU = 16.0
GLA_CHUNK = 64
GLA_KW = GLA_HEADS * GLA_DK
GLA_VW = GLA_HEADS * GLA_DV
CONV_DIM = 512
CONV_WIDTH = 3
RANK_PAD = 128
SWA_HEADS = 16
SWA_KV_HEADS = 4
SWA_GROUP = SWA_HEADS // SWA_KV_HEADS
SWA_HEAD_DIM = 64
WINDOW = 128
ROT_DIM = 16
ROPE_THETA = 500000.0
SWA_GW = SWA_GROUP * SWA_HEAD_DIM
XA_HEADS = 4
XA_HEAD_DIM = 256
FFN_HIDDEN = 2816
FFN_CHUNK = 256

LANES = 128
SUBLANES = 8
VMEM_LIMIT = 56 * 1024 * 1024

GLA_TM = 256
GLA_ROWS = 4
SWA_TQ = 512
SWA_ROWS = 2
SWA_LOOKAHEAD = 4
XA_TM = 1024
XA_SUB = 2
FFN_TM = 512


def _rms(x, gain):
    return x * lax.rsqrt(jnp.mean(x * x, axis=-1, keepdims=True) + RMS_EPS) * gain


def _dot(a, b):
    return jnp.dot(a, b, preferred_element_type=F32)


def _dot_nt(a, b):
    return lax.dot_general(a, b, (((1,), (1,)), ((), ())), preferred_element_type=F32)


def _dot_tn(a, b):
    return lax.dot_general(a, b, (((0,), (0,)), ((), ())), preferred_element_type=F32)


def _split_bf16(x):
    hi = x.astype(BF16)
    lo = (x - hi.astype(F32)).astype(BF16)
    return hi, lo


def _const_spec(shape):
    return pl.BlockSpec(shape, lambda *_: (0,) * len(shape))


def _layer_spec(shape, layer):
    return pl.BlockSpec((1,) + tuple(shape), lambda *_: (layer,) + (0,) * len(shape))


def _params(*semantics, flags=None):
    return pltpu.CompilerParams(dimension_semantics=semantics, vmem_limit_bytes=VMEM_LIMIT, flags=flags)


def _ffn_body(x_ref, gain_ref, wgu_ref, wd_ref, o_ref, h_ref, acc_ref):
    x = x_ref[...]
    h_ref[...] = _rms(x, gain_ref[0]).astype(BF16)
    acc_ref[...] = x
    for c in range(FFN_HIDDEN // FFN_CHUNK):
        lo = c * FFN_CHUNK
        h = h_ref[...]
        g = _dot(h, wgu_ref[0, :, lo:lo + FFN_CHUNK])
        u = _dot(h, wgu_ref[0, :, FFN_HIDDEN + lo:FFN_HIDDEN + lo + FFN_CHUNK])
        a = (g / (1.0 + jnp.exp(-g)) * u).astype(BF16)
        acc_ref[...] += _dot(a, wd_ref[0, lo:lo + FFN_CHUNK, :])
    o_ref[...] = acc_ref[...]


def _ffn(x2, layer, ffn_norm, w_gate_up, w_down):
    m, d = x2.shape
    tm = min(FFN_TM, m)
    return pl.pallas_call(
        _ffn_body,
        out_shape=jax.ShapeDtypeStruct((m, d), F32),
        grid=(m // tm,),
        in_specs=[
            pl.BlockSpec((tm, d), lambda i: (i, 0)),
            _layer_spec((1, d), layer),
            _layer_spec((d, 2 * FFN_HIDDEN), layer),
            _layer_spec((FFN_HIDDEN, d), layer),
        ],
        out_specs=pl.BlockSpec((tm, d), lambda i: (i, 0)),
        scratch_shapes=[pltpu.VMEM((tm, d), BF16), pltpu.VMEM((tm, d), F32)],
        compiler_params=_params("parallel"),
        name="ffn",
    )(x2, ffn_norm[:, None, :], w_gate_up, w_down)


def _xa_kv_body(mem_ref, gain_ref, wkv_ref, kgain_ref, k_ref, vt_ref):
    mh = _rms(mem_ref[0], gain_ref[0]).astype(BF16)
    kv = _dot(mh, wkv_ref[0])
    for h in range(XA_HEADS):
        lo = h * XA_HEAD_DIM
        k_ref[0, :, lo:lo + XA_HEAD_DIM] = _rms(kv[:, lo:lo + XA_HEAD_DIM], kgain_ref[0]).astype(BF16)
    vt_ref[0] = kv[:, D_MODEL:].T.astype(BF16)


def _xa_body(x_ref, gain_ref, wq_t_ref, qgain_ref, k_ref, vt_ref, wo_ref, o_ref, att_ref):
    hd = XA_HEAD_DIM
    mem = k_ref.shape[1]
    ts = x_ref.shape[1] // XA_SUB
    subs = range(XA_SUB)
    xs = [x_ref[0, i * ts:(i + 1) * ts, :] for i in subs]
    q_t = [_dot_nt(wq_t_ref[0], _rms(x, gain_ref[0]).astype(BF16)) for x in xs]
    qscale = hd ** -0.5 * float(np.log2(np.e))
    ones_rows = jnp.ones((2 * SUBLANES, mem), BF16)

    units = [(i, head) for head in range(XA_HEADS) for i in subs]
    scores = []
    for i, head in units:
        z = q_t[i][head * hd:(head + 1) * hd, :]
        ss = jnp.sum(z * z, axis=0, keepdims=True)
        z = z * (lax.rsqrt(ss * (1.0 / hd) + RMS_EPS) * qscale) * qgain_ref[0]
        scores.append(_dot(k_ref[0, :, head * hd:(head + 1) * hd], z.astype(BF16)))

    for (i, head), s in zip(units, scores):
        p = jnp.exp2(s - jnp.max(s, axis=0, keepdims=True)).astype(BF16)
        v_aug = jnp.concatenate([vt_ref[0, head * hd:(head + 1) * hd, :], ones_rows], axis=0)
        o_aug = _dot(v_aug, p)
        o = o_aug[0:hd, :] * (1.0 / o_aug[hd:hd + 1, :])
        att_ref[i, head * hd:(head + 1) * hd, :] = o.astype(BF16)
    for i in subs:
        o_ref[0, i * ts:(i + 1) * ts, :] = xs[i] + _dot_tn(att_ref[i], wo_ref[0])


def _cross_attention(x, mem, layer, mem_norm, xa_norm, wq_t, wkv, q_gain, k_gain, wo):
    b, t, d = x.shape
    m = mem.shape[1]
    k, vt = pl.pallas_call(
        _xa_kv_body,
        out_shape=(jax.ShapeDtypeStruct((b, m, d), BF16), jax.ShapeDtypeStruct((b, d, m), BF16)),
        grid=(b,),
        in_specs=[
            pl.BlockSpec((1, m, d), lambda i: (i, 0, 0)),
            _layer_spec((1, d), layer),
            _layer_spec((d, 2 * d), layer),
            _layer_spec((1, XA_HEAD_DIM), layer),
        ],
        out_specs=(pl.BlockSpec((1, m, d), lambda i: (i, 0, 0)), pl.BlockSpec((1, d, m), lambda i: (i, 0, 0))),
        compiler_params=_params("parallel"),
        name="xa_kv",
    )(mem, mem_norm[:, None, :], wkv, k_gain[:, None, :])
    tm = min(XA_TM, t)
    return pl.pallas_call(
        _xa_body,
        out_shape=jax.ShapeDtypeStruct((b, t, d), F32),
        grid=(b, t // tm),
        in_specs=[
            pl.BlockSpec((1, tm, d), lambda i, j: (i, j, 0)),
            _layer_spec((1, d), layer),
            _layer_spec((d, d), layer),
            _layer_spec((XA_HEAD_DIM, 1), layer),
            pl.BlockSpec((1, m, d), lambda i, j: (i, 0, 0)),
            pl.BlockSpec((1, d, m), lambda i, j: (i, 0, 0)),
            _layer_spec((d, d), layer),
        ],
        out_specs=pl.BlockSpec((1, tm, d), lambda i, j: (i, j, 0)),
        scratch_shapes=[pltpu.VMEM((XA_SUB, d, tm // XA_SUB), BF16)],
        compiler_params=_params("parallel", "parallel"),
        name="xa",
    )(x, xa_norm[:, None, :], wq_t, q_gain[:, :, None], k, vt, wo)


_GLA_Q, _GLA_K, _GLA_V, _GLA_GO, _GLA_CB, _GLA_CC, _GLA_CI, _GLA_GL = (
    0, 256, 512, 1024, 1536, 2048, 2560, 3072)
GLA_PROJ_W = _GLA_GL + RANK_PAD


def _gla_body(x_ref, gain_ref, win_ref, wg2_ref, gbias_ref, ogain_ref, convw_ref, wout_ref,
              tri_ref, ones_ref, hmask_ref, smask_ref, o_ref,
              st_ref, ubuf_ref, utail_ref, cat_ref):
    rows, tm = x_ref.shape[0], x_ref.shape[1]
    seqs = range(rows)

    @pl.when(pl.program_id(1) == 0)
    def _():
        st_ref[...] = jnp.zeros_like(st_ref)
        utail_ref[...] = jnp.zeros_like(utail_ref)

    xs, projs, bs, b_lasts = [], [], [], []
    q_in, k_in, k_out, decay, v = [], [], [], [], []

    def project(r):
        xs.append(x_ref[r])
        projs.append(_dot(_rms(xs[r], gain_ref[...]).astype(BF16), win_ref[...]))

    def gates(r):
        logits = _dot(projs[r][:, _GLA_GL:_GLA_GL + RANK_PAD].astype(BF16), wg2_ref[...]) + gbias_ref[...]
        log_a = (jnp.minimum(logits, 0.0) - jnp.log(1.0 + jnp.exp(-jnp.abs(logits)))) * (1.0 / GLA_TAU)
        la_hi, la_lo = _split_bf16(log_a)
        bs.append(_dot(tri_ref[...], la_hi) + _dot(tri_ref[...], la_lo))
        b_lasts.append(_dot(ones_ref[...], la_hi) + _dot(ones_ref[...], la_lo))

    def vector_stage(r):
        proj, b, b_last = projs[r], bs[r], b_lasts[r]
        q = proj[:, _GLA_Q:_GLA_Q + GLA_KW] * (GLA_DK ** -0.5)
        k = proj[:, _GLA_K:_GLA_K + GLA_KW]
        v.append(proj[:, _GLA_V:_GLA_V + GLA_VW].astype(BF16))
        q_in.append((q * jnp.exp(b)).astype(BF16))
        k_in.append((k * jnp.exp(-b)).astype(BF16))
        k_out.append((k * jnp.exp(b_last - b)).astype(BF16))
        decay.append(jnp.exp(b_last))
        u = proj[:, _GLA_CC:_GLA_CC + CONV_DIM] * proj[:, _GLA_CI:_GLA_CI + CONV_DIM]
        ubuf_ref[r, 0:SUBLANES, :] = utail_ref[r]
        ubuf_ref[r, SUBLANES:SUBLANES + tm, :] = u
        y = (convw_ref[0:1, :] * ubuf_ref[r, SUBLANES - 2:SUBLANES - 2 + tm, :]
             + convw_ref[1:2, :] * ubuf_ref[r, SUBLANES - 1:SUBLANES - 1 + tm, :]
             + convw_ref[2:3, :] * u)
        cat_ref[r, :, GLA_VW:GLA_VW + CONV_DIM] = (proj[:, _GLA_CB:_GLA_CB + CONV_DIM] * y).astype(BF16)

    project(0)
    gates(0)
    for r in range(1, rows):
        project(r)
        vector_stage(r - 1)
        gates(r)
    vector_stage(rows - 1)

    o_inter = [[] for _ in seqs]
    for c in range(tm // GLA_CHUNK):
        r0 = c * GLA_CHUNK
        for r in seqs:
            st = st_ref[r]
            o_inter[r].append(_dot_nt(q_in[r][r0:r0 + GLA_CHUNK], st.astype(BF16)))
            kv_t = _dot_tn(v[r][r0:r0 + GLA_CHUNK], k_out[r][r0:r0 + GLA_CHUNK])
            st_ref[r] = st * decay[r][r0:r0 + 1, :] + jnp.where(smask_ref[...] > 0.5, kv_t, 0.0)
    o_inter = [jnp.concatenate(o, axis=0) for o in o_inter]

    causal = tri_ref[...] > 0.5
    for hd in range(GLA_HEADS):
        lo = hd * GLA_DV
        for r in seqs:
            q_h = jnp.where(hmask_ref[hd:hd + 1, :] > 0.5, q_in[r], jnp.zeros_like(q_in[r]))
            scores = jnp.where(causal, _dot_nt(q_h, k_in[r]), 0.0)
            o_h = _dot(scores.astype(BF16), v[r][:, lo:lo + GLA_DV]) + o_inter[r][:, lo:lo + GLA_DV]
            g_h = projs[r][:, _GLA_GO + lo:_GLA_GO + lo + GLA_DV]
            y_h = _rms(o_h, ogain_ref[...]) * (g_h / (1.0 + jnp.exp(-g_h)))
            cat_ref[r, :, lo:lo + GLA_DV] = y_h.astype(BF16)

    for r in seqs:
        o_ref[r] = xs[r] + _dot(cat_ref[r], wout_ref[...])

    @pl.when(pl.program_id(1) + 1 < pl.num_programs(1))
    def _():
        for r in seqs:
            utail_ref[r] = ubuf_ref[r, tm:tm + SUBLANES, :]


def _gla_layer(x, gain, w_in, w_gate2, gate_bias, out_gain, conv_w, w_out):
    b, t, d = x.shape
    tm = min(GLA_TM, t)
    rows = GLA_ROWS
    widths = [GLA_KW, GLA_KW, GLA_VW, GLA_RANK, GLA_VW, CONV_DIM, CONV_DIM, CONV_DIM]
    cuts = [int(c) for c in np.cumsum(widths)[:-1]]
    wq, wk, wv, wgl, wgo, wcb, wcc, wci = jnp.split(w_in, cuts, axis=1)
    wgl = jnp.pad(wgl, ((0, 0), (0, RANK_PAD - GLA_RANK)))
    w_in_r = jnp.concatenate([wq, wk, wv, wgo, wcb, wcc, wci, wgl], axis=1).astype(BF16)
    wg2 = jnp.pad(w_gate2, ((0, RANK_PAD - GLA_RANK), (0, 0))).astype(BF16)

    r = np.arange(tm)
    same_chunk = (r[:, None] // GLA_CHUNK) == (r[None, :] // GLA_CHUNK)
    tri = jnp.asarray(same_chunk & (r[None, :] <= r[:, None]), BF16)
    ones = jnp.asarray(same_chunk, BF16)
    hmask = jnp.asarray(np.arange(GLA_KW)[None, :] // GLA_DK == np.arange(GLA_HEADS)[:, None], F32)
    smask = jnp.asarray(
        (np.arange(GLA_VW)[:, None] // GLA_DV) == (np.arange(GLA_KW)[None, :] // GLA_DK), F32)

    return pl.pallas_call(
        _gla_body,
        out_shape=jax.ShapeDtypeStruct((b, t, d), F32),
        grid=(b // rows, t // tm),
        in_specs=[
            pl.BlockSpec((rows, tm, d), lambda i, j: (i, j, 0)),
            _const_spec((1, d)),
            _const_spec((d, GLA_PROJ_W)),
            _const_spec((RANK_PAD, GLA_KW)),
            _const_spec((1, GLA_KW)),
            _const_spec((1, GLA_DV)),
            _const_spec((CONV_WIDTH, CONV_DIM)),
            _const_spec((GLA_VW + CONV_DIM, d)),
            _const_spec((tm, tm)),
            _const_spec((tm, tm)),
            _const_spec((GLA_HEADS, GLA_KW)),
            _const_spec((GLA_VW, GLA_KW)),
        ],
        out_specs=pl.BlockSpec((rows, tm, d), lambda i, j: (i, j, 0)),
        scratch_shapes=[
            pltpu.VMEM((rows, GLA_VW, GLA_KW), F32),
            pltpu.VMEM((rows, tm + SUBLANES, CONV_DIM), F32),
            pltpu.VMEM((rows, SUBLANES, CONV_DIM), F32),
            pltpu.VMEM((rows, tm, GLA_VW + CONV_DIM), BF16),
        ],
        compiler_params=_params("parallel", "arbitrary"),
        name="gla_conv",
    )(x, gain.reshape(1, d), w_in_r, wg2, gate_bias.reshape(1, GLA_KW), out_gain.reshape(1, GLA_DV),
      conv_w, w_out.astype(BF16), tri, ones, hmask, smask)


def _norm_rope_heads(z, n_heads, gain, cos_f, sin_f):
    hd, half = SWA_HEAD_DIM, ROT_DIM // 2
    out = []
    for head in range(n_heads):
        zh = z[head * hd:(head + 1) * hd, :]
        ss = jnp.sum(zh * zh, axis=0, keepdims=True)
        zh = zh * lax.rsqrt(ss * (1.0 / hd) + RMS_EPS) * gain
        x1, x2 = zh[0:half], zh[half:ROT_DIM]
        out.append(jnp.concatenate(
            [x1 * cos_f - x2 * sin_f, x2 * cos_f + x1 * sin_f, zh[ROT_DIM:]], axis=0))
    return out


def _swa_body(x_ref, gain_ref, wqkv_t_ref, qgain_ref, kgain_ref, cos_ref, sin_ref,
              band_ref, band0_ref, sink_ref, wout_ref, o_ref,
              qpad_ref, kbuf_ref, vbuf_ref, kprev_ref, vprev_ref, att_ref):
    n_seq, tq = x_ref.shape[0], x_ref.shape[1]
    seqs = range(n_seq)
    nb = tq // WINDOW
    n = pl.program_id(1)
    hd = SWA_HEAD_DIM
    qw = SWA_HEADS * hd
    kw = SWA_KV_HEADS * hd

    @pl.when(n == 0)
    def _():
        qpad_ref[...] = jnp.zeros_like(qpad_ref)
        kprev_ref[...] = jnp.zeros_like(kprev_ref)
        vprev_ref[...] = jnp.zeros_like(vprev_ref)

    xs = [x_ref[r] for r in seqs]
    qkv_t = [_dot_nt(wqkv_t_ref[...], _rms(x, gain_ref[...]).astype(BF16)) for x in xs]

    for r in seqs:
        cos_f, sin_f = cos_ref[r], sin_ref[r]
        k_heads = _norm_rope_heads(qkv_t[r][qw:qw + kw], SWA_KV_HEADS, kgain_ref[...], cos_f, sin_f)
        kbuf_ref[r, 0:WINDOW, :] = kprev_ref[r]
        kbuf_ref[r, WINDOW:WINDOW + tq, :] = jnp.concatenate(k_heads, axis=0).T.astype(BF16)
        vbuf_ref[r, :, 0:WINDOW] = vprev_ref[r]
        vbuf_ref[r, :, WINDOW:WINDOW + tq] = qkv_t[r][qw + kw:, :].astype(BF16)
        q_heads = _norm_rope_heads(qkv_t[r][:qw], SWA_HEADS, qgain_ref[...], cos_f, sin_f)
        for head, z in enumerate(q_heads):
            g, j = divmod(head, SWA_GROUP)
            z = z.astype(BF16)
            for i in range(nb):
                qpad_ref[r, g, i, g * hd:(g + 1) * hd, j * WINDOW:(j + 1) * WINDOW] = (
                    z[:, i * WINDOW:(i + 1) * WINDOW])

    bias0 = jnp.where(n > 0, band_ref[...], band0_ref[...])

    units = [(r, g, i) for g in range(SWA_KV_HEADS) for i in range(nb) for r in seqs]

    def scores(unit):
        r, g, i = unit
        return _dot(kbuf_ref[r, i * WINDOW:(i + 2) * WINDOW, :], qpad_ref[r, g, i])

    ones_rows = jnp.ones((2 * SUBLANES, 2 * WINDOW), BF16)
    pending = [scores(unit) for unit in units[:SWA_LOOKAHEAD]]
    for u, (r, g, i) in enumerate(units):
        r0 = i * WINDOW
        s = pending.pop(0) + (bias0 if i == 0 else band_ref[...])
        sink = sink_ref[g]
        m = jnp.maximum(jnp.max(s, axis=0, keepdims=True), sink)
        if u + SWA_LOOKAHEAD < len(units):
            pending.append(scores(units[u + SWA_LOOKAHEAD]))
            m = m + jnp.minimum(jnp.abs(pending[-1][0:1, :]), 0.0)
        p = jnp.exp2(s - m).astype(BF16)
        v_aug = jnp.concatenate([vbuf_ref[r, g * hd:(g + 1) * hd, r0:r0 + 2 * WINDOW], ones_rows], axis=0)
        o_aug = _dot(v_aug, p)
        denom = o_aug[hd:hd + 1, :] + jnp.exp2(sink - m)
        o = o_aug[0:hd, :] * (1.0 / denom)
        for j in range(SWA_GROUP):
            head = g * SWA_GROUP + j
            att_ref[r, head * hd:(head + 1) * hd, r0:r0 + WINDOW] = (
                o[:, j * WINDOW:(j + 1) * WINDOW].astype(BF16))

    for r in seqs:
        o_ref[r] = xs[r] + _dot_tn(att_ref[r], wout_ref[...])

    @pl.when(n + 1 < pl.num_programs(1))
    def _():
        for r in seqs:
            kprev_ref[r] = kbuf_ref[r, tq:tq + WINDOW, :]
            vprev_ref[r] = vbuf_ref[r, :, tq:tq + WINDOW]


def _swa_layer(x, positions, gain, w_qkv, q_gain, k_gain, sinks, w_out):
    b, t, d = x.shape
    tq = min(SWA_TQ, t)
    nb = tq // WINDOW
    hd = SWA_HEAD_DIM
    qw = SWA_HEADS * hd
    kw = SWA_KV_HEADS * hd
    inv_freq = ROPE_THETA ** (-jnp.arange(0, ROT_DIM, 2, dtype=F32) / ROT_DIM)
    ang = positions.astype(F32)[:, None, :] * inv_freq[None, :, None]
    cos_f, sin_f = jnp.cos(ang), jnp.sin(ang)
    log2e = float(np.log2(np.e))
    qgain_f = jnp.broadcast_to((q_gain * (hd ** -0.5 * log2e))[:, None], (hd, tq))
    kgain_f = jnp.broadcast_to(k_gain[:, None], (hd, tq))
    rows = SWA_GROUP * WINDOW
    si = np.arange(2 * WINDOW)[:, None]
    qi = np.arange(WINDOW)[None, :] + WINDOW
    band_ok = (qi - si >= 0) & (qi - si < WINDOW)
    band = jnp.asarray(np.tile(np.where(band_ok, 0.0, -np.inf), (1, SWA_GROUP)), F32)
    band0 = jnp.asarray(np.tile(np.where(band_ok & (si >= WINDOW), 0.0, -np.inf), (1, SWA_GROUP)), F32)
    sink_cols = jnp.repeat((sinks.astype(F32) * log2e).reshape(SWA_KV_HEADS, 1, SWA_GROUP), WINDOW, axis=2)

    n_seq = SWA_ROWS
    row_spec = pl.BlockSpec((n_seq, tq, d), lambda i, j: (i, j, 0))
    tab_spec = pl.BlockSpec((n_seq, ROT_DIM // 2, tq), lambda i, j: (i, 0, j))
    return pl.pallas_call(
        _swa_body,
        out_shape=jax.ShapeDtypeStruct((b, t, d), F32),
        grid=(b // n_seq, t // tq),
        in_specs=[
            row_spec,
            _const_spec((1, d)),
            _const_spec((qw + 2 * kw, d)),
            _const_spec((hd, tq)),
            _const_spec((hd, tq)),
            tab_spec, tab_spec,
            _const_spec((2 * WINDOW, rows)),
            _const_spec((2 * WINDOW, rows)),
            _const_spec((SWA_KV_HEADS, 1, rows)),
            _const_spec((d, d)),
        ],
        out_specs=row_spec,
        scratch_shapes=[
            pltpu.VMEM((n_seq, SWA_KV_HEADS, nb, kw, rows), BF16),
            pltpu.VMEM((n_seq, WINDOW + tq, kw), BF16),
            pltpu.VMEM((n_seq, kw, WINDOW + tq), BF16),
            pltpu.VMEM((n_seq, WINDOW, kw), BF16),
            pltpu.VMEM((n_seq, kw, WINDOW), BF16),
            pltpu.VMEM((n_seq, qw, tq), BF16),
        ],
        compiler_params=_params("parallel", "arbitrary"),
        name="swa",
    )(x, gain.reshape(1, d), w_qkv.T.astype(BF16), qgain_f, kgain_f, cos_f, sin_f,
      band, band0, sink_cols, w_out.astype(BF16))


def kernel(x, mem, positions, mix_norm, hyb_w_in, gla_w_gate2, gla_gate_bias, gla_out_gain, conv_w,
           hyb_w_out, swa_w_qkv, swa_q_gain, swa_k_gain, swa_sinks, swa_w_out, mem_norm, xa_norm, xa_wq,
           xa_wkv, xa_q_gain, xa_k_gain, xa_wo, ffn_norm, ffn_w_gate_up, ffn_w_down):
    b, t, d = x.shape
    depth = mix_norm.shape[0]
    xa_wq_t = jnp.swapaxes(xa_wq, 1, 2).astype(BF16)
    xa_wkv_b, xa_wo_b = xa_wkv.astype(BF16), xa_wo.astype(BF16)
    ffn_wgu_b, ffn_wd_b = ffn_w_gate_up.astype(BF16), ffn_w_down.astype(BF16)
    for layer in range(depth):
        i = layer // 2
        if layer % 2 == 0:
            x = _gla_layer(x, mix_norm[layer], hyb_w_in[i], gla_w_gate2[i], gla_gate_bias[i],
                           gla_out_gain[i], conv_w[i], hyb_w_out[i])
        else:
            x = _swa_layer(x, positions, mix_norm[layer], swa_w_qkv[i], swa_q_gain[i], swa_k_gain[i],
                           swa_sinks[i], swa_w_out[i])
        x = _cross_attention(x, mem, layer, mem_norm, xa_norm, xa_wq_t, xa_wkv_b, xa_q_gain, xa_k_gain,
                             xa_wo_b)
        x = _ffn(x.reshape(b * t, d), layer, ffn_norm, ffn_wgu_b, ffn_wd_b).reshape(b, t, d)
    return x
```

```python
import functools

import numpy as np
import jax
import jax.numpy as jnp
from jax import lax
from jax.experimental import pallas as pl
from jax.experimental.pallas import tpu as pltpu

F32 = jnp.float32
BF16 = jnp.bfloat16

D_MODEL = 1024
RMS_EPS = 1e-6
GLA_HEADS = 4
GLA_DK = 64
GLA_DV = 128
GLA_RANK = 16
GLA_TAU = 16.0
GLA_CHUNK = 64
GLA_KW = GLA_HEADS * GLA_DK
GLA_VW = GLA_HEADS * GLA_DV
CONV_DIM = 512
CONV_WIDTH = 3
RANK_PAD = 128
SWA_HEADS = 16
SWA_KV_HEADS = 4
SWA_GROUP = SWA_HEADS // SWA_KV_HEADS
SWA_HEAD_DIM = 64
WINDOW = 128
ROT_DIM = 16
ROPE_THETA = 500000.0
SWA_GW = SWA_GROUP * SWA_HEAD_DIM
XA_HEADS = 4
XA_HEAD_DIM = 256
FFN_HIDDEN = 2816
FFN_CHUNK = 256

LANES = 128
SUBLANES = 8
VMEM_LIMIT = 56 * 1024 * 1024

GLA_TM = 256
GLA_ROWS = 4
SWA_TQ = 512
SWA_ROWS = 2
SWA_WAVE = 8
SWA_LOOKAHEAD = 1
XA_TM = 1024
XA_SUB = 2
FFN_TM = 1024
FFN_SUB = 2


def _rms(x, gain):
    return x * lax.rsqrt(jnp.mean(x * x, axis=-1, keepdims=True) + RMS_EPS) * gain


def _dot(a, b):
    return jnp.dot(a, b, preferred_element_type=F32)


def _dot_nt(a, b):
    return lax.dot_general(a, b, (((1,), (1,)), ((), ())), preferred_element_type=F32)


def _dot_tn(a, b):
    return lax.dot_general(a, b, (((0,), (0,)), ((), ())), preferred_element_type=F32)


def _split_bf16(x):
    hi = x.astype(BF16)
    lo = (x - hi.astype(F32)).astype(BF16)
    return hi, lo


def _const_spec(shape):
    return pl.BlockSpec(shape, lambda *_: (0,) * len(shape))


def _layer_spec(shape, layer):
    return pl.BlockSpec((1,) + tuple(shape), lambda *_: (layer,) + (0,) * len(shape))


def _params(*semantics):
    return pltpu.CompilerParams(dimension_semantics=semantics, vmem_limit_bytes=VMEM_LIMIT)


def _ffn_body(x_ref, gain_ref, wgu_ref, wd_ref, o_ref, h_ref, acc_ref):
    ts = x_ref.shape[0] // FFN_SUB
    for i in range(FFN_SUB):
        x = x_ref[i * ts:(i + 1) * ts, :]
        h_ref[i] = _rms(x, gain_ref[0]).astype(BF16)
        acc_ref[i] = x
    for c in range(FFN_HIDDEN // FFN_CHUNK):
        lo = c * FFN_CHUNK
        for i in range(FFN_SUB):
            h = h_ref[i]
            g = _dot(h, wgu_ref[0, :, lo:lo + FFN_CHUNK])
            u = _dot(h, wgu_ref[0, :, FFN_HIDDEN + lo:FFN_HIDDEN + lo + FFN_CHUNK])
            a = (g / (1.0 + jnp.exp(-g)) * u).astype(BF16)
            acc_ref[i] += _dot(a, wd_ref[0, lo:lo + FFN_CHUNK, :])
    for i in range(FFN_SUB):
        o_ref[i * ts:(i + 1) * ts, :] = acc_ref[i]


def _ffn(x2, layer, ffn_norm, w_gate_up, w_down):
    m, d = x2.shape
    tm = min(FFN_TM, m)
    return pl.pallas_call(
        _ffn_body,
        out_shape=jax.ShapeDtypeStruct((m, d), F32),
        grid=(m // tm,),
        in_specs=[
            pl.BlockSpec((tm, d), lambda i: (i, 0)),
            _layer_spec((1, d), layer),
            _layer_spec((d, 2 * FFN_HIDDEN), layer),
            _layer_spec((FFN_HIDDEN, d), layer),
        ],
        out_specs=pl.BlockSpec((tm, d), lambda i: (i, 0)),
        scratch_shapes=[pltpu.VMEM((FFN_SUB, tm // FFN_SUB, d), BF16),
                        pltpu.VMEM((FFN_SUB, tm // FFN_SUB, d), F32)],
        compiler_params=_params("parallel"),
        name="ffn",
    )(x2, ffn_norm[:, None, :], w_gate_up, w_down)


def _xa_kv_body(mem_ref, gain_ref, wkv_ref, kgain_ref, k_ref, vt_ref):
    mh = _rms(mem_ref[0], gain_ref[0]).astype(BF16)
    kv = _dot(mh, wkv_ref[0])
    for h in range(XA_HEADS):
        lo = h * XA_HEAD_DIM
        k_ref[0, :, lo:lo + XA_HEAD_DIM] = _rms(kv[:, lo:lo + XA_HEAD_DIM], kgain_ref[0]).astype(BF16)
    vt_ref[0] = kv[:, D_MODEL:].T.astype(BF16)


def _xa_body(x_ref, gain_ref, wq_t_ref, qgain_ref, k_ref, vt_ref, wo_ref, o_ref, att_ref):
    hd = XA_HEAD_DIM
    mem = k_ref.shape[1]
    ts = x_ref.shape[1] // XA_SUB
    subs = range(XA_SUB)
    xs = [x_ref[0, i * ts:(i + 1) * ts, :] for i in subs]
    q_t = [_dot_nt(wq_t_ref[0], _rms(x, gain_ref[0]).astype(BF16)) for x in xs]
    qscale = hd ** -0.5 * float(np.log2(np.e))
    ones_rows = jnp.ones((2 * SUBLANES, mem), BF16)

    units = [(i, head) for head in range(XA_HEADS) for i in subs]
    scores = []
    for i, head in units:
        z = q_t[i][head * hd:(head + 1) * hd, :]
        ss = jnp.sum(z * z, axis=0, keepdims=True)
        z = z * (lax.rsqrt(ss * (1.0 / hd) + RMS_EPS) * qscale) * qgain_ref[0]
        scores.append(_dot(k_ref[0, :, head * hd:(head + 1) * hd], z.astype(BF16)))

    probs = [jnp.exp2(s - jnp.max(s, axis=0, keepdims=True)).astype(BF16) for s in scores]
    for (i, head), p in zip(units, probs):
        v_aug = jnp.concatenate([vt_ref[0, head * hd:(head + 1) * hd, :], ones_rows], axis=0)
        o_aug = _dot(v_aug, p)
        o = o_aug[0:hd, :] * (1.0 / o_aug[hd:hd + 1, :])
        att_ref[i, head * hd:(head + 1) * hd, :] = o.astype(BF16)
    for i in subs:
        o_ref[0, i * ts:(i + 1) * ts, :] = xs[i] + _dot_tn(att_ref[i], wo_ref[0])


def _cross_attention(x, mem, layer, mem_norm, xa_norm, wq_t, wkv, q_gain, k_gain, wo):
    b, t, d = x.shape
    m = mem.shape[1]
    k, vt = pl.pallas_call(
        _xa_kv_body,
        out_shape=(jax.ShapeDtypeStruct((b, m, d), BF16), jax.ShapeDtypeStruct((b, d, m), BF16)),
        grid=(b,),
        in_specs=[
            pl.BlockSpec((1, m, d), lambda i: (i, 0, 0)),
            _layer_spec((1, d), layer),
            _layer_spec((d, 2 * d), layer),
            _layer_spec((1, XA_HEAD_DIM), layer),
        ],
        out_specs=(pl.BlockSpec((1, m, d), lambda i: (i, 0, 0)), pl.BlockSpec((1, d, m), lambda i: (i, 0, 0))),
        compiler_params=_params("parallel"),
        name="xa_kv",
    )(mem, mem_norm[:, None, :], wkv, k_gain[:, None, :])
    tm = min(XA_TM, t)
    return pl.pallas_call(
        _xa_body,
        out_shape=jax.ShapeDtypeStruct((b, t, d), F32),
        grid=(b, t // tm),
        in_specs=[
            pl.BlockSpec((1, tm, d), lambda i, j: (i, j, 0)),
            _layer_spec((1, d), layer),
            _layer_spec((d, d), layer),
            _layer_spec((XA_HEAD_DIM, 1), layer),
            pl.BlockSpec((1, m, d), lambda i, j: (i, 0, 0)),
            pl.BlockSpec((1, d, m), lambda i, j: (i, 0, 0)),
            _layer_spec((d, d), layer),
        ],
        out_specs=pl.BlockSpec((1, tm, d), lambda i, j: (i, j, 0)),
        scratch_shapes=[pltpu.VMEM((XA_SUB, d, tm // XA_SUB), BF16)],
        compiler_params=_params("parallel", "parallel"),
        name="xa",
    )(x, xa_norm[:, None, :], wq_t, q_gain[:, :, None], k, vt, wo)


_GLA_Q, _GLA_K, _GLA_V, _GLA_GO, _GLA_CB, _GLA_CC, _GLA_CI, _GLA_GL = (
    0, 256, 512, 1024, 1536, 2048, 2560, 3072)
GLA_PROJ_W = _GLA_GL + RANK_PAD


def _gla_body(x_ref, gain_ref, win_ref, wg2_ref, gbias_ref, ogain_ref, convw_ref, wout_ref,
              tri_ref, ones_ref, hmask_ref, smask_ref, o_ref,
              st_ref, ubuf_ref, utail_ref, cat_ref):
    rows, tm = x_ref.shape[0], x_ref.shape[1]
    seqs = range(rows)

    @pl.when(pl.program_id(1) == 0)
    def _():
        st_ref[...] = jnp.zeros_like(st_ref)
        utail_ref[...] = jnp.zeros_like(utail_ref)

    xs, projs, bs, b_lasts = [], [], [], []
    q_in, k_in, k_out, decay, v = [], [], [], [], []

    def project(r):
        xs.append(x_ref[r])
        projs.append(_dot(_rms(xs[r], gain_ref[...]).astype(BF16), win_ref[...]))

    def gates(r):
        logits = _dot(projs[r][:, _GLA_GL:_GLA_GL + RANK_PAD].astype(BF16), wg2_ref[...]) + gbias_ref[...]
        log_a = (jnp.minimum(logits, 0.0) - jnp.log(1.0 + jnp.exp(-jnp.abs(logits)))) * (1.0 / GLA_TAU)
        la_hi, la_lo = _split_bf16(log_a)
        bs.append(_dot(tri_ref[...], la_hi) + _dot(tri_ref[...], la_lo))
        b_lasts.append(_dot(ones_ref[...], la_hi) + _dot(ones_ref[...], la_lo))

    def vector_stage(r):
        proj, b, b_last = projs[r], bs[r], b_lasts[r]
        q = proj[:, _GLA_Q:_GLA_Q + GLA_KW] * (GLA_DK ** -0.5)
        k = proj[:, _GLA_K:_GLA_K + GLA_KW]
        v.append(proj[:, _GLA_V:_GLA_V + GLA_VW].astype(BF16))
        q_in.append((q * jnp.exp(b)).astype(BF16))
        k_in.append((k * jnp.exp(-b)).astype(BF16))
        k_out.append((k * jnp.exp(b_last - b)).astype(BF16))
        decay.append(jnp.exp(b_last))
        u = proj[:, _GLA_CC:_GLA_CC + CONV_DIM] * proj[:, _GLA_CI:_GLA_CI + CONV_DIM]
        ubuf_ref[r, 0:SUBLANES, :] = utail_ref[r]
        ubuf_ref[r, SUBLANES:SUBLANES + tm, :] = u
        y = (convw_ref[0:1, :] * ubuf_ref[r, SUBLANES - 2:SUBLANES - 2 + tm, :]
             + convw_ref[1:2, :] * ubuf_ref[r, SUBLANES - 1:SUBLANES - 1 + tm, :]
             + convw_ref[2:3, :] * u)
        cat_ref[r, :, GLA_VW:GLA_VW + CONV_DIM] = (proj[:, _GLA_CB:_GLA_CB + CONV_DIM] * y).astype(BF16)

    project(0)
    gates(0)
    for r in range(1, rows):
        project(r)
        vector_stage(r - 1)
        gates(r)
    vector_stage(rows - 1)

    o_inter = [[] for _ in seqs]
    for c in range(tm // GLA_CHUNK):
        r0 = c * GLA_CHUNK
        for r in seqs:
            st = st_ref[r]
            o_inter[r].append(_dot_nt(q_in[r][r0:r0 + GLA_CHUNK], st.astype(BF16)))
            kv_t = _dot_tn(v[r][r0:r0 + GLA_CHUNK], k_out[r][r0:r0 + GLA_CHUNK])
            st_ref[r] = st * decay[r][r0:r0 + 1, :] + jnp.where(smask_ref[...] > 0.5, kv_t, 0.0)
    o_inter = [jnp.concatenate(o, axis=0) for o in o_inter]

    causal = tri_ref[...] > 0.5
    for hd in range(GLA_HEADS):
        lo = hd * GLA_DV
        for r in seqs:
            q_h = jnp.where(hmask_ref[hd:hd + 1, :] > 0.5, q_in[r], jnp.zeros_like(q_in[r]))
            scores = jnp.where(causal, _dot_nt(q_h, k_in[r]), 0.0)
            o_h = _dot(scores.astype(BF16), v[r][:, lo:lo + GLA_DV]) + o_inter[r][:, lo:lo + GLA_DV]
            g_h = projs[r][:, _GLA_GO + lo:_GLA_GO + lo + GLA_DV]
            y_h = _rms(o_h, ogain_ref[...]) * (g_h / (1.0 + jnp.exp(-g_h)))
            cat_ref[r, :, lo:lo + GLA_DV] = y_h.astype(BF16)

    for r in seqs:
        o_ref[r] = xs[r] + _dot(cat_ref[r], wout_ref[...])

    @pl.when(pl.program_id(1) + 1 < pl.num_programs(1))
    def _():
        for r in seqs:
            utail_ref[r] = ubuf_ref[r, tm:tm + SUBLANES, :]


def _gla_layer(x, gain, w_in, w_gate2, gate_bias, out_gain, conv_w, w_out):
    b, t, d = x.shape
    tm = min(GLA_TM, t)
    rows = GLA_ROWS
    widths = [GLA_KW, GLA_KW, GLA_VW, GLA_RANK, GLA_VW, CONV_DIM, CONV_DIM, CONV_DIM]
    cuts = [int(c) for c in np.cumsum(widths)[:-1]]
    wq, wk, wv, wgl, wgo, wcb, wcc, wci = jnp.split(w_in, cuts, axis=1)
    wgl = jnp.pad(wgl, ((0, 0), (0, RANK_PAD - GLA_RANK)))
    w_in_r = jnp.concatenate([wq, wk, wv, wgo, wcb, wcc, wci, wgl], axis=1).astype(BF16)
    wg2 = jnp.pad(w_gate2, ((0, RANK_PAD - GLA_RANK), (0, 0))).astype(BF16)

    r = np.arange(tm)
    same_chunk = (r[:, None] // GLA_CHUNK) == (r[None, :] // GLA_CHUNK)
    tri = jnp.asarray(same_chunk & (r[None, :] <= r[:, None]), BF16)
    ones = jnp.asarray(same_chunk, BF16)
    hmask = jnp.asarray(np.arange(GLA_KW)[None, :] // GLA_DK == np.arange(GLA_HEADS)[:, None], F32)
    smask = jnp.asarray(
        (np.arange(GLA_VW)[:, None] // GLA_DV) == (np.arange(GLA_KW)[None, :] // GLA_DK), F32)

    return pl.pallas_call(
        _gla_body,
        out_shape=jax.ShapeDtypeStruct((b, t, d), F32),
        grid=(b // rows, t // tm),
        in_specs=[
            pl.BlockSpec((rows, tm, d), lambda i, j: (i, j, 0)),
            _const_spec((1, d)),
            _const_spec((d, GLA_PROJ_W)),
            _const_spec((RANK_PAD, GLA_KW)),
            _const_spec((1, GLA_KW)),
            _const_spec((1, GLA_DV)),
            _const_spec((CONV_WIDTH, CONV_DIM)),
            _const_spec((GLA_VW + CONV_DIM, d)),
            _const_spec((tm, tm)),
            _const_spec((tm, tm)),
            _const_spec((GLA_HEADS, GLA_KW)),
            _const_spec((GLA_VW, GLA_KW)),
        ],
        out_specs=pl.BlockSpec((rows, tm, d), lambda i, j: (i, j, 0)),
        scratch_shapes=[
            pltpu.VMEM((rows, GLA_VW, GLA_KW), F32),
            pltpu.VMEM((rows, tm + SUBLANES, CONV_DIM), F32),
            pltpu.VMEM((rows, SUBLANES, CONV_DIM), F32),
            pltpu.VMEM((rows, tm, GLA_VW + CONV_DIM), BF16),
        ],
        compiler_params=_params("parallel", "arbitrary"),
        name="gla_conv",
    )(x, gain.reshape(1, d), w_in_r, wg2, gate_bias.reshape(1, GLA_KW), out_gain.reshape(1, GLA_DV),
      conv_w, w_out.astype(BF16), tri, ones, hmask, smask)


def _norm_rope_heads(z, n_heads, gain, cos_f, sin_f):
    hd, half = SWA_HEAD_DIM, ROT_DIM // 2
    out = []
    for head in range(n_heads):
        zh = z[head * hd:(head + 1) * hd, :]
        ss = jnp.sum(zh * zh, axis=0, keepdims=True)
        zh = zh * lax.rsqrt(ss * (1.0 / hd) + RMS_EPS) * gain
        x1, x2 = zh[0:half], zh[half:ROT_DIM]
        out.append(jnp.concatenate(
            [x1 * cos_f - x2 * sin_f, x2 * cos_f + x1 * sin_f, zh[ROT_DIM:]], axis=0))
    return out


def _swa_body(x_ref, gain_ref, wqkv_t_ref, qgain_ref, kgain_ref, cos_ref, sin_ref,
              band_ref, band0_ref, sink_ref, wout_ref, o_ref,
              qpad_ref, kbuf_ref, vbuf_ref, kprev_ref, vprev_ref, att_ref):
    n_seq, tq = x_ref.shape[0], x_ref.shape[1]
    seqs = range(n_seq)
    nb = tq // WINDOW
    n = pl.program_id(1)
    hd = SWA_HEAD_DIM
    qw = SWA_HEADS * hd
    kw = SWA_KV_HEADS * hd

    @pl.when(n == 0)
    def _():
        qpad_ref[...] = jnp.zeros_like(qpad_ref)
        kprev_ref[...] = jnp.zeros_like(kprev_ref)
        vprev_ref[...] = jnp.zeros_like(vprev_ref)

    xs = [x_ref[r] for r in seqs]
    qkv_t = [_dot_nt(wqkv_t_ref[...], _rms(x, gain_ref[...]).astype(BF16)) for x in xs]

    for r in seqs:
        cos_f, sin_f = cos_ref[r], sin_ref[r]
        k_heads = _norm_rope_heads(qkv_t[r][qw:qw + kw], SWA_KV_HEADS, kgain_ref[...], cos_f, sin_f)
        kbuf_ref[r, 0:WINDOW, :] = kprev_ref[r]
        kbuf_ref[r, WINDOW:WINDOW + tq, :] = jnp.concatenate(k_heads, axis=0).T.astype(BF16)
        vbuf_ref[r, :, 0:WINDOW] = vprev_ref[r]
        vbuf_ref[r, :, WINDOW:WINDOW + tq] = qkv_t[r][qw + kw:, :].astype(BF16)
        q_heads = _norm_rope_heads(qkv_t[r][:qw], SWA_HEADS, qgain_ref[...], cos_f, sin_f)
        for head, z in enumerate(q_heads):
            g, j = divmod(head, SWA_GROUP)
            z = z.astype(BF16)
            for i in range(nb):
                qpad_ref[r, g, i, g * hd:(g + 1) * hd, j * WINDOW:(j + 1) * WINDOW] = (
                    z[:, i * WINDOW:(i + 1) * WINDOW])

    bias0 = jnp.where(n > 0, band_ref[...], band0_ref[...])

    units = [(r, g, i) for g in range(SWA_KV_HEADS) for i in range(nb) for r in seqs]

    def scores(unit):
        r, g, i = unit
        return _dot(kbuf_ref[r, i * WINDOW:(i + 2) * WINDOW, :], qpad_ref[r, g, i])

    ones_rows = jnp.ones((2 * SUBLANES, 2 * WINDOW), BF16)
    waves = [units[w:w + SWA_WAVE] for w in range(0, len(units), SWA_WAVE)]
    pending = [[scores(unit) for unit in wave] for wave in waves[:SWA_LOOKAHEAD]]
    for w, wave in enumerate(waves):
        s_wave = pending.pop(0)
        if w + SWA_LOOKAHEAD < len(waves):
            pending.append([scores(unit) for unit in waves[w + SWA_LOOKAHEAD]])
        stats, probs = [], []
        for (r, g, i), s in zip(wave, s_wave):
            s = s + (bias0 if i == 0 else band_ref[...])
            sink = sink_ref[g]
            m = jnp.maximum(jnp.max(s, axis=0, keepdims=True), sink)
            if w + SWA_LOOKAHEAD < len(waves):
                m = m + jnp.minimum(jnp.abs(pending[-1][0][0:1, :]), 0.0)
            probs.append(jnp.exp2(s - m).astype(BF16))
            stats.append((sink, m))
        for (r, g, i), p, (sink, m) in zip(wave, probs, stats):
            r0 = i * WINDOW
            v_aug = jnp.concatenate(
                [vbuf_ref[r, g * hd:(g + 1) * hd, r0:r0 + 2 * WINDOW], ones_rows], axis=0)
            o_aug = _dot(v_aug, p)
            denom = o_aug[hd:hd + 1, :] + jnp.exp2(sink - m)
            o = o_aug[0:hd, :] * (1.0 / denom)
            for j in range(SWA_GROUP):
                head = g * SWA_GROUP + j
                att_ref[r, head * hd:(head + 1) * hd, r0:r0 + WINDOW] = (
                    o[:, j * WINDOW:(j + 1) * WINDOW].astype(BF16))

    for r in seqs:
        o_ref[r] = xs[r] + _dot_tn(att_ref[r], wout_ref[...])

    @pl.when(n + 1 < pl.num_programs(1))
    def _():
        for r in seqs:
            kprev_ref[r] = kbuf_ref[r, tq:tq + WINDOW, :]
            vprev_ref[r] = vbuf_ref[r, :, tq:tq + WINDOW]


def _swa_layer(x, positions, gain, w_qkv, q_gain, k_gain, sinks, w_out):
    b, t, d = x.shape
    tq = min(SWA_TQ, t)
    nb = tq // WINDOW
    hd = SWA_HEAD_DIM
    qw = SWA_HEADS * hd
    kw = SWA_KV_HEADS * hd
    inv_freq = ROPE_THETA ** (-jnp.arange(0, ROT_DIM, 2, dtype=F32) / ROT_DIM)
    ang = positions.astype(F32)[:, None, :] * inv_freq[None, :, None]
    cos_f, sin_f = jnp.cos(ang), jnp.sin(ang)
    log2e = float(np.log2(np.e))
    qgain_f = jnp.broadcast_to((q_gain * (hd ** -0.5 * log2e))[:, None], (hd, tq))
    kgain_f = jnp.broadcast_to(k_gain[:, None], (hd, tq))
    rows = SWA_GROUP * WINDOW
    si = np.arange(2 * WINDOW)[:, None]
    qi = np.arange(WINDOW)[None, :] + WINDOW
    band_ok = (qi - si >= 0) & (qi - si < WINDOW)
    band = jnp.asarray(np.tile(np.where(band_ok, 0.0, -np.inf), (1, SWA_GROUP)), F32)
    band0 = jnp.asarray(np.tile(np.where(band_ok & (si >= WINDOW), 0.0, -np.inf), (1, SWA_GROUP)), F32)
    sink_cols = jnp.repeat((sinks.astype(F32) * log2e).reshape(SWA_KV_HEADS, 1, SWA_GROUP), WINDOW, axis=2)

    n_seq = SWA_ROWS
    row_spec = pl.BlockSpec((n_seq, tq, d), lambda i, j: (i, j, 0))
    tab_spec = pl.BlockSpec((n_seq, ROT_DIM // 2, tq), lambda i, j: (i, 0, j))
    return pl.pallas_call(
        _swa_body,
        out_shape=jax.ShapeDtypeStruct((b, t, d), F32),
        grid=(b // n_seq, t // tq),
        in_specs=[
            row_spec,
            _const_spec((1, d)),
            _const_spec((qw + 2 * kw, d)),
            _const_spec((hd, tq)),
            _const_spec((hd, tq)),
            tab_spec, tab_spec,
            _const_spec((2 * WINDOW, rows)),
            _const_spec((2 * WINDOW, rows)),
            _const_spec((SWA_KV_HEADS, 1, rows)),
            _const_spec((d, d)),
        ],
        out_specs=row_spec,
        scratch_shapes=[
            pltpu.VMEM((n_seq, SWA_KV_HEADS, nb, kw, rows), BF16),
            pltpu.VMEM((n_seq, WINDOW + tq, kw), BF16),
            pltpu.VMEM((n_seq, kw, WINDOW + tq), BF16),
            pltpu.VMEM((n_seq, WINDOW, kw), BF16),
            pltpu.VMEM((n_seq, kw, WINDOW), BF16),
            pltpu.VMEM((n_seq, qw, tq), BF16),
        ],
        compiler_params=_params("parallel", "arbitrary"),
        name="swa",
    )(x, gain.reshape(1, d), w_qkv.T.astype(BF16), qgain_f, kgain_f, cos_f, sin_f,
      band, band0, sink_cols, w_out.astype(BF16))


def kernel(x, mem, positions, mix_norm, hyb_w_in, gla_w_gate2, gla_gate_bias, gla_out_gain, conv_w,
           hyb_w_out, swa_w_qkv, swa_q_gain, swa_k_gain, swa_sinks, swa_w_out, mem_norm, xa_norm, xa_wq,
           xa_wkv, xa_q_gain, xa_k_gain, xa_wo, ffn_norm, ffn_w_gate_up, ffn_w_down):
    b, t, d = x.shape
    depth = mix_norm.shape[0]
    xa_wq_t = jnp.swapaxes(xa_wq, 1, 2).astype(BF16)
    xa_wkv_b, xa_wo_b = xa_wkv.astype(BF16), xa_wo.astype(BF16)
    ffn_wgu_b, ffn_wd_b = ffn_w_gate_up.astype(BF16), ffn_w_down.astype(BF16)
    for layer in range(depth):
        i = layer // 2
        if layer % 2 == 0:
            x = _gla_layer(x, mix_norm[layer], hyb_w_in[i], gla_w_gate2[i], gla_gate_bias[i],
                           gla_out_gain[i], conv_w[i], hyb_w_out[i])
        else:
            x = _swa_layer(x, positions, mix_norm[layer], swa_w_qkv[i], swa_q_gain[i], swa_k_gain[i],
                           swa_sinks[i], swa_w_out[i])
        x = _cross_attention(x, mem, layer, mem_norm, xa_norm, xa_wq_t, xa_wkv_b, xa_q_gain, xa_k_gain,
                             xa_wo_b)
        x = _ffn(x.reshape(b * t, d), layer, ffn_norm, ffn_wgu_b, ffn_wd_b).reshape(b, t, d)
    return x
```

```python
import functools

import numpy as np
import jax
import jax.numpy as jnp
from jax import lax
from jax.experimental import pallas as pl
from jax.experimental.pallas import tpu as pltpu

F32 = jnp.float32
BF16 = jnp.bfloat16

D_MODEL = 1024
RMS_EPS = 1e-6
GLA_HEADS = 4
GLA_DK = 64
GLA_DV = 128
GLA_RANK = 16
GLA_TAU = 16.0
GLA_CHUNK = 64
GLA_KW = GLA_HEADS * GLA_DK
GLA_VW = GLA_HEADS * GLA_DV
CONV_DIM = 512
CONV_WIDTH = 3
RANK_PAD = 128
SWA_HEADS = 16
SWA_KV_HEADS = 4
SWA_GROUP = SWA_HEADS // SWA_KV_HEADS
SWA_HEAD_DIM = 64
WINDOW = 128
ROT_DIM = 16
ROPE_THETA = 500000.0
SWA_GW = SWA_GROUP * SWA_HEAD_DIM
XA_HEADS = 4
XA_HEAD_DIM = 256
FFN_HIDDEN = 2816
FFN_CHUNK = 256

LANES = 128
SUBLANES = 8
VMEM_LIMIT = 56 * 1024 * 1024

GLA_TM = 256
GLA_ROWS = 4
SWA_TQ = 512
SWA_ROWS = 2
SWA_WAVE = 8
SWA_LOOKAHEAD = 1
XA_TM = 1024
XA_KV_ROWS = 2
XA_SUB = 2
FFN_TM = 1024
FFN_SUB = 2


def _rms(x, gain):
    return x * lax.rsqrt(jnp.mean(x * x, axis=-1, keepdims=True) + RMS_EPS) * gain


def _dot(a, b):
    return jnp.dot(a, b, preferred_element_type=F32)


def _dot_nt(a, b):
    return lax.dot_general(a, b, (((1,), (1,)), ((), ())), preferred_element_type=F32)


def _dot_tn(a, b):
    return lax.dot_general(a, b, (((0,), (0,)), ((), ())), preferred_element_type=F32)


def _split_bf16(x):
    hi = x.astype(BF16)
    lo = (x - hi.astype(F32)).astype(BF16)
    return hi, lo


def _const_spec(shape):
    return pl.BlockSpec(shape, lambda *_: (0,) * len(shape))


def _layer_spec(shape, layer):
    return pl.BlockSpec((1,) + tuple(shape), lambda *_: (layer,) + (0,) * len(shape))


def _params(*semantics):
    return pltpu.CompilerParams(dimension_semantics=semantics, vmem_limit_bytes=VMEM_LIMIT)


def _ffn_body(x_ref, gain_ref, wgu_ref, wd_ref, o_ref, h_ref, acc_ref):
    ts = x_ref.shape[0] // FFN_SUB
    for i in range(FFN_SUB):
        x = x_ref[i * ts:(i + 1) * ts, :]
        h_ref[i] = _rms(x, gain_ref[0]).astype(BF16)
        acc_ref[i] = x
    for c in range(FFN_HIDDEN // FFN_CHUNK):
        lo = c * FFN_CHUNK
        for i in range(FFN_SUB):
            h = h_ref[i]
            g = _dot(h, wgu_ref[0, :, lo:lo + FFN_CHUNK])
            u = _dot(h, wgu_ref[0, :, FFN_HIDDEN + lo:FFN_HIDDEN + lo + FFN_CHUNK])
            a = (g / (1.0 + jnp.exp(-g)) * u).astype(BF16)
            acc_ref[i] += _dot(a, wd_ref[0, lo:lo + FFN_CHUNK, :])
    for i in range(FFN_SUB):
        o_ref[i * ts:(i + 1) * ts, :] = acc_ref[i]


def _ffn(x2, layer, ffn_norm, w_gate_up, w_down):
    m, d = x2.shape
    tm = min(FFN_TM, m)
    return pl.pallas_call(
        _ffn_body,
        out_shape=jax.ShapeDtypeStruct((m, d), F32),
        grid=(m // tm,),
        in_specs=[
            pl.BlockSpec((tm, d), lambda i: (i, 0)),
            _layer_spec((1, d), layer),
            _layer_spec((d, 2 * FFN_HIDDEN), layer),
            _layer_spec((FFN_HIDDEN, d), layer),
        ],
        out_specs=pl.BlockSpec((tm, d), lambda i: (i, 0)),
        scratch_shapes=[pltpu.VMEM((FFN_SUB, tm // FFN_SUB, d), BF16),
                        pltpu.VMEM((FFN_SUB, tm // FFN_SUB, d), F32)],
        compiler_params=_params("parallel"),
        name="ffn",
    )(x2, ffn_norm[:, None, :], w_gate_up, w_down)


def _xa_kv_body(mem_ref, gain_ref, wkv_ref, kgain_ref, k_ref, vt_ref):
    rows, mem = mem_ref.shape[0], mem_ref.shape[1]
    mh = jnp.concatenate([_rms(mem_ref[r], gain_ref[0]).astype(BF16) for r in range(rows)], axis=0)
    kv = _dot(mh, wkv_ref[0])
    for r in range(rows):
        kv_r = kv[r * mem:(r + 1) * mem]
        for h in range(XA_HEADS):
            lo = h * XA_HEAD_DIM
            k_ref[r, :, lo:lo + XA_HEAD_DIM] = _rms(kv_r[:, lo:lo + XA_HEAD_DIM], kgain_ref[0]).astype(BF16)
        vt_ref[r] = kv_r[:, D_MODEL:].T.astype(BF16)


def _xa_body(x_ref, gain_ref, wq_t_ref, qgain_ref, k_ref, vt_ref, wo_ref, o_ref, att_ref):
    hd = XA_HEAD_DIM
    mem = k_ref.shape[1]
    ts = x_ref.shape[1] // XA_SUB
    subs = range(XA_SUB)
    xs = [x_ref[0, i * ts:(i + 1) * ts, :] for i in subs]
    q_t = [_dot_nt(wq_t_ref[0], _rms(x, gain_ref[0]).astype(BF16)) for x in xs]
    qscale = hd ** -0.5 * float(np.log2(np.e))
    ones_rows = jnp.ones((2 * SUBLANES, mem), BF16)

    units = [(i, head) for head in range(XA_HEADS) for i in subs]
    scores = []
    for i, head in units:
        z = q_t[i][head * hd:(head + 1) * hd, :]
        ss = jnp.sum(z * z, axis=0, keepdims=True)
        z = z * (lax.rsqrt(ss * (1.0 / hd) + RMS_EPS) * qscale) * qgain_ref[0]
        scores.append(_dot(k_ref[0, :, head * hd:(head + 1) * hd], z.astype(BF16)))

    probs = [jnp.exp2(s - jnp.max(s, axis=0, keepdims=True)).astype(BF16) for s in scores]
    for (i, head), p in zip(units, probs):
        v_aug = jnp.concatenate([vt_ref[0, head * hd:(head + 1) * hd, :], ones_rows], axis=0)
        o_aug = _dot(v_aug, p)
        o = o_aug[0:hd, :] * (1.0 / o_aug[hd:hd + 1, :])
        att_ref[i, head * hd:(head + 1) * hd, :] = o.astype(BF16)
    for i in subs:
        o_ref[0, i * ts:(i + 1) * ts, :] = xs[i] + _dot_tn(att_ref[i], wo_ref[0])


def _cross_attention(x, mem, layer, mem_norm, xa_norm, wq_t, wkv, q_gain, k_gain, wo):
    b, t, d = x.shape
    m = mem.shape[1]
    k, vt = pl.pallas_call(
        _xa_kv_body,
        out_shape=(jax.ShapeDtypeStruct((b, m, d), BF16), jax.ShapeDtypeStruct((b, d, m), BF16)),
        grid=(b // XA_KV_ROWS,),
        in_specs=[
            pl.BlockSpec((XA_KV_ROWS, m, d), lambda i: (i, 0, 0)),
            _layer_spec((1, d), layer),
            _layer_spec((d, 2 * d), layer),
            _layer_spec((1, XA_HEAD_DIM), layer),
        ],
        out_specs=(pl.BlockSpec((XA_KV_ROWS, m, d), lambda i: (i, 0, 0)),
                   pl.BlockSpec((XA_KV_ROWS, d, m), lambda i: (i, 0, 0))),
        compiler_params=_params("parallel"),
        name="xa_kv",
    )(mem, mem_norm[:, None, :], wkv, k_gain[:, None, :])
    tm = min(XA_TM, t)
    return pl.pallas_call(
        _xa_body,
        out_shape=jax.ShapeDtypeStruct((b, t, d), F32),
        grid=(b, t // tm),
        in_specs=[
            pl.BlockSpec((1, tm, d), lambda i, j: (i, j, 0)),
            _layer_spec((1, d), layer),
            _layer_spec((d, d), layer),
            _layer_spec((XA_HEAD_DIM, 1), layer),
            pl.BlockSpec((1, m, d), lambda i, j: (i, 0, 0)),
            pl.BlockSpec((1, d, m), lambda i, j: (i, 0, 0)),
            _layer_spec((d, d), layer),
        ],
        out_specs=pl.BlockSpec((1, tm, d), lambda i, j: (i, j, 0)),
        scratch_shapes=[pltpu.VMEM((XA_SUB, d, tm // XA_SUB), BF16)],
        compiler_params=_params("parallel", "parallel"),
        name="xa",
    )(x, xa_norm[:, None, :], wq_t, q_gain[:, :, None], k, vt, wo)


_GLA_Q, _GLA_K, _GLA_V, _GLA_GO, _GLA_CB, _GLA_CC, _GLA_CI, _GLA_GL = (
    0, 256, 512, 1024, 1536, 2048, 2560, 3072)
GLA_PROJ_W = _GLA_GL + RANK_PAD


def _gla_body(x_ref, gain_ref, win_ref, wg2_ref, gbias_ref, ogain_ref, convw_ref, wout_ref,
              tri_ref, ones_ref, hmask_ref, smask_ref, o_ref,
              st_ref, ubuf_ref, utail_ref, cat_ref):
    rows, tm = x_ref.shape[0], x_ref.shape[1]
    seqs = range(rows)

    @pl.when(pl.program_id(1) == 0)
    def _():
        st_ref[...] = jnp.zeros_like(st_ref)
        utail_ref[...] = jnp.zeros_like(utail_ref)

    xs, projs, bs, b_lasts = [], [], [], []
    q_in, k_in, k_out, decay, v = [], [], [], [], []

    def project(r):
        xs.append(x_ref[r])
        projs.append(_dot(_rms(xs[r], gain_ref[...]).astype(BF16), win_ref[...]))

    def gates(r):
        logits = _dot(projs[r][:, _GLA_GL:_GLA_GL + RANK_PAD].astype(BF16), wg2_ref[...]) + gbias_ref[...]
        log_a = (jnp.minimum(logits, 0.0) - jnp.log(1.0 + jnp.exp(-jnp.abs(logits)))) * (1.0 / GLA_TAU)
        la_hi, la_lo = _split_bf16(log_a)
        bs.append(_dot(tri_ref[...], la_hi) + _dot(tri_ref[...], la_lo))
        b_lasts.append(_dot(ones_ref[...], la_hi) + _dot(ones_ref[...], la_lo))

    def vector_stage(r):
        proj, b, b_last = projs[r], bs[r], b_lasts[r]
        q = proj[:, _GLA_Q:_GLA_Q + GLA_KW] * (GLA_DK ** -0.5)
        k = proj[:, _GLA_K:_GLA_K + GLA_KW]
        v.append(proj[:, _GLA_V:_GLA_V + GLA_VW].astype(BF16))
        q_in.append((q * jnp.exp(b)).astype(BF16))
        k_in.append((k * jnp.exp(-b)).astype(BF16))
        k_out.append((k * jnp.exp(b_last - b)).astype(BF16))
        decay.append(jnp.exp(b_last))
        u = proj[:, _GLA_CC:_GLA_CC + CONV_DIM] * proj[:, _GLA_CI:_GLA_CI + CONV_DIM]
        ubuf_ref[r, 0:SUBLANES, :] = utail_ref[r]
        ubuf_ref[r, SUBLANES:SUBLANES + tm, :] = u
        y = (convw_ref[0:1, :] * ubuf_ref[r, SUBLANES - 2:SUBLANES - 2 + tm, :]
             + convw_ref[1:2, :] * ubuf_ref[r, SUBLANES - 1:SUBLANES - 1 + tm, :]
             + convw_ref[2:3, :] * u)
        cat_ref[r, :, GLA_VW:GLA_VW + CONV_DIM] = (proj[:, _GLA_CB:_GLA_CB + CONV_DIM] * y).astype(BF16)

    project(0)
    gates(0)
    for r in range(1, rows):
        project(r)
        vector_stage(r - 1)
        gates(r)
    vector_stage(rows - 1)

    o_inter = [[] for _ in seqs]
    for c in range(tm // GLA_CHUNK):
        r0 = c * GLA_CHUNK
        for r in seqs:
            st = st_ref[r]
            o_inter[r].append(_dot_nt(q_in[r][r0:r0 + GLA_CHUNK], st.astype(BF16)))
            kv_t = _dot_tn(v[r][r0:r0 + GLA_CHUNK], k_out[r][r0:r0 + GLA_CHUNK])
            st_ref[r] = st * decay[r][r0:r0 + 1, :] + jnp.where(smask_ref[...] > 0.5, kv_t, 0.0)
    o_inter = [jnp.concatenate(o, axis=0) for o in o_inter]

    causal = tri_ref[...] > 0.5
    for hd in range(GLA_HEADS):
        lo = hd * GLA_DV
        for r in seqs:
            q_h = jnp.where(hmask_ref[hd:hd + 1, :] > 0.5, q_in[r], jnp.zeros_like(q_in[r]))
            scores = jnp.where(causal, _dot_nt(q_h, k_in[r]), 0.0)
            o_h = _dot(scores.astype(BF16), v[r][:, lo:lo + GLA_DV]) + o_inter[r][:, lo:lo + GLA_DV]
            g_h = projs[r][:, _GLA_GO + lo:_GLA_GO + lo + GLA_DV]
            y_h = _rms(o_h, ogain_ref[...]) * (g_h / (1.0 + jnp.exp(-g_h)))
            cat_ref[r, :, lo:lo + GLA_DV] = y_h.astype(BF16)

    for r in seqs:
        o_ref[r] = xs[r] + _dot(cat_ref[r], wout_ref[...])

    @pl.when(pl.program_id(1) + 1 < pl.num_programs(1))
    def _():
        for r in seqs:
            utail_ref[r] = ubuf_ref[r, tm:tm + SUBLANES, :]


def _gla_layer(x, gain, w_in, w_gate2, gate_bias, out_gain, conv_w, w_out):
    b, t, d = x.shape
    tm = min(GLA_TM, t)
    rows = GLA_ROWS
    widths = [GLA_KW, GLA_KW, GLA_VW, GLA_RANK, GLA_VW, CONV_DIM, CONV_DIM, CONV_DIM]
    cuts = [int(c) for c in np.cumsum(widths)[:-1]]
    wq, wk, wv, wgl, wgo, wcb, wcc, wci = jnp.split(w_in, cuts, axis=1)
    wgl = jnp.pad(wgl, ((0, 0), (0, RANK_PAD - GLA_RANK)))
    w_in_r = jnp.concatenate([wq, wk, wv, wgo, wcb, wcc, wci, wgl], axis=1).astype(BF16)
    wg2 = jnp.pad(w_gate2, ((0, RANK_PAD - GLA_RANK), (0, 0))).astype(BF16)

    r = np.arange(tm)
    same_chunk = (r[:, None] // GLA_CHUNK) == (r[None, :] // GLA_CHUNK)
    tri = jnp.asarray(same_chunk & (r[None, :] <= r[:, None]), BF16)
    ones = jnp.asarray(same_chunk, BF16)
    hmask = jnp.asarray(np.arange(GLA_KW)[None, :] // GLA_DK == np.arange(GLA_HEADS)[:, None], F32)
    smask = jnp.asarray(
        (np.arange(GLA_VW)[:, None] // GLA_DV) == (np.arange(GLA_KW)[None, :] // GLA_DK), F32)

    return pl.pallas_call(
        _gla_body,
        out_shape=jax.ShapeDtypeStruct((b, t, d), F32),
        grid=(b // rows, t // tm),
        in_specs=[
            pl.BlockSpec((rows, tm, d), lambda i, j: (i, j, 0)),
            _const_spec((1, d)),
            _const_spec((d, GLA_PROJ_W)),
            _const_spec((RANK_PAD, GLA_KW)),
            _const_spec((1, GLA_KW)),
            _const_spec((1, GLA_DV)),
            _const_spec((CONV_WIDTH, CONV_DIM)),
            _const_spec((GLA_VW + CONV_DIM, d)),
            _const_spec((tm, tm)),
            _const_spec((tm, tm)),
            _const_spec((GLA_HEADS, GLA_KW)),
            _const_spec((GLA_VW, GLA_KW)),
        ],
        out_specs=pl.BlockSpec((rows, tm, d), lambda i, j: (i, j, 0)),
        scratch_shapes=[
            pltpu.VMEM((rows, GLA_VW, GLA_KW), F32),
            pltpu.VMEM((rows, tm + SUBLANES, CONV_DIM), F32),
            pltpu.VMEM((rows, SUBLANES, CONV_DIM), F32),
            pltpu.VMEM((rows, tm, GLA_VW + CONV_DIM), BF16),
        ],
        compiler_params=_params("parallel", "arbitrary"),
        name="gla_conv",
    )(x, gain.reshape(1, d), w_in_r, wg2, gate_bias.reshape(1, GLA_KW), out_gain.reshape(1, GLA_DV),
      conv_w, w_out.astype(BF16), tri, ones, hmask, smask)


def _norm_rope_heads(z, n_heads, gain, cos_f, sin_f):
    hd, half = SWA_HEAD_DIM, ROT_DIM // 2
    out = []
    for head in range(n_heads):
        zh = z[head * hd:(head + 1) * hd, :]
        ss = jnp.sum(zh * zh, axis=0, keepdims=True)
        zh = zh * lax.rsqrt(ss * (1.0 / hd) + RMS_EPS) * gain
        x1, x2 = zh[0:half], zh[half:ROT_DIM]
        out.append(jnp.concatenate(
            [x1 * cos_f - x2 * sin_f, x2 * cos_f + x1 * sin_f, zh[ROT_DIM:]], axis=0))
    return out


def _swa_body(x_ref, gain_ref, wqkv_t_ref, qgain_ref, kgain_ref, cos_ref, sin_ref,
              band_ref, band0_ref, sink_ref, wout_ref, o_ref,
              qpad_ref, kbuf_ref, vbuf_ref, kprev_ref, vprev_ref, att_ref):
    n_seq, tq = x_ref.shape[0], x_ref.shape[1]
    seqs = range(n_seq)
    nb = tq // WINDOW
    n = pl.program_id(1)
    hd = SWA_HEAD_DIM
    qw = SWA_HEADS * hd
    kw = SWA_KV_HEADS * hd

    @pl.when((pl.program_id(0) == 0) & (n == 0))
    def _():
        qpad_ref[...] = jnp.zeros_like(qpad_ref)

    @pl.when(n == 0)
    def _():
        kprev_ref[...] = jnp.zeros_like(kprev_ref)
        vprev_ref[...] = jnp.zeros_like(vprev_ref)

    xs = [x_ref[r] for r in seqs]
    qkv_t = [_dot_nt(wqkv_t_ref[...], _rms(x, gain_ref[...]).astype(BF16)) for x in xs]

    for r in seqs:
        cos_f, sin_f = cos_ref[r], sin_ref[r]
        k_heads = _norm_rope_heads(qkv_t[r][qw:qw + kw], SWA_KV_HEADS, kgain_ref[...], cos_f, sin_f)
        kbuf_ref[r, 0:WINDOW, :] = kprev_ref[r]
        kbuf_ref[r, WINDOW:WINDOW + tq, :] = jnp.concatenate(k_heads, axis=0).T.astype(BF16)
        vbuf_ref[r, :, 0:WINDOW] = vprev_ref[r]
        vbuf_ref[r, :, WINDOW:WINDOW + tq] = qkv_t[r][qw + kw:, :].astype(BF16)
        q_heads = _norm_rope_heads(qkv_t[r][:qw], SWA_HEADS, qgain_ref[...], cos_f, sin_f)
        for head, z in enumerate(q_heads):
            g, j = divmod(head, SWA_GROUP)
            z = z.astype(BF16)
            for i in range(nb):
                qpad_ref[r, g, i, g * hd:(g + 1) * hd, j * WINDOW:(j + 1) * WINDOW] = (
                    z[:, i * WINDOW:(i + 1) * WINDOW])

    bias0 = jnp.where(n > 0, band_ref[...], band0_ref[...])

    units = [(r, g, i) for g in range(SWA_KV_HEADS) for i in range(nb) for r in seqs]

    def scores(unit):
        r, g, i = unit
        return _dot(kbuf_ref[r, i * WINDOW:(i + 2) * WINDOW, :], qpad_ref[r, g, i])

    ones_rows = jnp.ones((2 * SUBLANES, 2 * WINDOW), BF16)
    waves = [units[w:w + SWA_WAVE] for w in range(0, len(units), SWA_WAVE)]
    pending = [[scores(unit) for unit in wave] for wave in waves[:SWA_LOOKAHEAD]]
    for w, wave in enumerate(waves):
        s_wave = pending.pop(0)
        if w + SWA_LOOKAHEAD < len(waves):
            pending.append([scores(unit) for unit in waves[w + SWA_LOOKAHEAD]])
        stats, probs = [], []
        for (r, g, i), s in zip(wave, s_wave):
            s = s + (bias0 if i == 0 else band_ref[...])
            sink = sink_ref[g]
            m = jnp.maximum(jnp.max(s, axis=0, keepdims=True), sink)
            if w + SWA_LOOKAHEAD < len(waves):
                m = m + jnp.minimum(jnp.abs(pending[-1][0][0:1, :]), 0.0)
            probs.append(jnp.exp2(s - m).astype(BF16))
            stats.append((sink, m))
        for (r, g, i), p, (sink, m) in zip(wave, probs, stats):
            r0 = i * WINDOW
            v_aug = jnp.concatenate(
                [vbuf_ref[r, g * hd:(g + 1) * hd, r0:r0 + 2 * WINDOW], ones_rows], axis=0)
            o_aug = _dot(v_aug, p)
            denom = o_aug[hd:hd + 1, :] + jnp.exp2(sink - m)
            o = o_aug[0:hd, :] * (1.0 / denom)
            for j in range(SWA_GROUP):
                head = g * SWA_GROUP + j
                att_ref[r, head * hd:(head + 1) * hd, r0:r0 + WINDOW] = (
                    o[:, j * WINDOW:(j + 1) * WINDOW].astype(BF16))

    for r in seqs:
        o_ref[r] = xs[r] + _dot_tn(att_ref[r], wout_ref[...])

    @pl.when(n + 1 < pl.num_programs(1))
    def _():
        for r in seqs:
            kprev_ref[r] = kbuf_ref[r, tq:tq + WINDOW, :]
            vprev_ref[r] = vbuf_ref[r, :, tq:tq + WINDOW]


def _swa_layer(x, positions, gain, w_qkv, q_gain, k_gain, sinks, w_out):
    b, t, d = x.shape
    tq = min(SWA_TQ, t)
    nb = tq // WINDOW
    hd = SWA_HEAD_DIM
    qw = SWA_HEADS * hd
    kw = SWA_KV_HEADS * hd
    inv_freq = ROPE_THETA ** (-jnp.arange(0, ROT_DIM, 2, dtype=F32) / ROT_DIM)
    ang = positions.astype(F32)[:, None, :] * inv_freq[None, :, None]
    cos_f, sin_f = jnp.cos(ang), jnp.sin(ang)
    log2e = float(np.log2(np.e))
    qgain_f = jnp.broadcast_to((q_gain * (hd ** -0.5 * log2e))[:, None], (hd, tq))
    kgain_f = jnp.broadcast_to(k_gain[:, None], (hd, tq))
    rows = SWA_GROUP * WINDOW
    si = np.arange(2 * WINDOW)[:, None]
    qi = np.arange(WINDOW)[None, :] + WINDOW
    band_ok = (qi - si >= 0) & (qi - si < WINDOW)
    band = jnp.asarray(np.tile(np.where(band_ok, 0.0, -np.inf), (1, SWA_GROUP)), F32)
    band0 = jnp.asarray(np.tile(np.where(band_ok & (si >= WINDOW), 0.0, -np.inf), (1, SWA_GROUP)), F32)
    sink_cols = jnp.repeat((sinks.astype(F32) * log2e).reshape(SWA_KV_HEADS, 1, SWA_GROUP), WINDOW, axis=2)

    n_seq = SWA_ROWS
    row_spec = pl.BlockSpec((n_seq, tq, d), lambda i, j: (i, j, 0))
    tab_spec = pl.BlockSpec((n_seq, ROT_DIM // 2, tq), lambda i, j: (i, 0, j))
    return pl.pallas_call(
        _swa_body,
        out_shape=jax.ShapeDtypeStruct((b, t, d), F32),
        grid=(b // n_seq, t // tq),
        in_specs=[
            row_spec,
            _const_spec((1, d)),
            _const_spec((qw + 2 * kw, d)),
            _const_spec((hd, tq)),
            _const_spec((hd, tq)),
            tab_spec, tab_spec,
            _const_spec((2 * WINDOW, rows)),
            _const_spec((2 * WINDOW, rows)),
            _const_spec((SWA_KV_HEADS, 1, rows)),
            _const_spec((d, d)),
        ],
        out_specs=row_spec,
        scratch_shapes=[
            pltpu.VMEM((n_seq, SWA_KV_HEADS, nb, kw, rows), BF16),
            pltpu.VMEM((n_seq, WINDOW + tq, kw), BF16),
            pltpu.VMEM((n_seq, kw, WINDOW + tq), BF16),
            pltpu.VMEM((n_seq, WINDOW, kw), BF16),
            pltpu.VMEM((n_seq, kw, WINDOW), BF16),
            pltpu.VMEM((n_seq, qw, tq), BF16),
        ],
        compiler_params=_params("arbitrary", "arbitrary"),
        name="swa",
    )(x, gain.reshape(1, d), w_qkv.T.astype(BF16), qgain_f, kgain_f, cos_f, sin_f,
      band, band0, sink_cols, w_out.astype(BF16))


def kernel(x, mem, positions, mix_norm, hyb_w_in, gla_w_gate2, gla_gate_bias, gla_out_gain, conv_w,
           hyb_w_out, swa_w_qkv, swa_q_gain, swa_k_gain, swa_sinks, swa_w_out, mem_norm, xa_norm, xa_wq,
           xa_wkv, xa_q_gain, xa_k_gain, xa_wo, ffn_norm, ffn_w_gate_up, ffn_w_down):
    b, t, d = x.shape
    depth = mix_norm.shape[0]
    xa_wq_t = jnp.swapaxes(xa_wq, 1, 2).astype(BF16)
    xa_wkv_b, xa_wo_b = xa_wkv.astype(BF16), xa_wo.astype(BF16)
    ffn_wgu_b, ffn_wd_b = ffn_w_gate_up.astype(BF16), ffn_w_down.astype(BF16)
    for layer in range(depth):
        i = layer // 2
        if layer % 2 == 0:
            x = _gla_layer(x, mix_norm[layer], hyb_w_in[i], gla_w_gate2[i], gla_gate_bias[i],
                           gla_out_gain[i], conv_w[i], hyb_w_out[i])
        else:
            x = _swa_layer(x, positions, mix_norm[layer], swa_w_qkv[i], swa_q_gain[i], swa_k_gain[i],
                           swa_sinks[i], swa_w_out[i])
        x = _cross_attention(x, mem, layer, mem_norm, xa_norm, xa_wq_t, xa_wkv_b, xa_q_gain, xa_k_gain,
                             xa_wo_b)
        x = _ffn(x.reshape(b * t, d), layer, ffn_norm, ffn_wgu_b, ffn_wd_b).reshape(b, t, d)
    return x
```

```python
import functools

import numpy as np
import jax
import jax.numpy as jnp
from jax import lax
from jax.experimental import pallas as pl
from jax.experimental.pallas import tpu as pltpu

F32 = jnp.float32
BF16 = jnp.bfloat16

D_MODEL = 1024
RMS_EPS = 1e-6
GLA_HEADS = 4
GLA_DK = 64
GLA_DV = 128
GLA_RANK = 16
GLA_TAU = 16.0
GLA_CHUNK = 64
GLA_KW = GLA_HEADS * GLA_DK
GLA_VW = GLA_HEADS * GLA_DV
CONV_DIM = 512
CONV_WIDTH = 3
RANK_PAD = 128
SWA_HEADS = 16
SWA_KV_HEADS = 4
SWA_GROUP = SWA_HEADS // SWA_KV_HEADS
SWA_HEAD_DIM = 64
WINDOW = 128
ROT_DIM = 16
ROPE_THETA = 500000.0
SWA_GW = SWA_GROUP * SWA_HEAD_DIM
XA_HEADS = 4
XA_HEAD_DIM = 256
FFN_HIDDEN = 2816
FFN_CHUNK = 256

LANES = 128
SUBLANES = 8
VMEM_LIMIT = 56 * 1024 * 1024

GLA_TM = 256
GLA_ROWS = 4
SWA_TQ = 512
SWA_ROWS = 2
SWA_WAVE = 8
SWA_LOOKAHEAD = 1
XA_TM = 1024
XA_KV_ROWS = 2
XA_SUB = 2
FFN_TM = 1024
FFN_SUB = 2


def _rms(x, gain):
    return x * lax.rsqrt(jnp.mean(x * x, axis=-1, keepdims=True) + RMS_EPS) * gain


def _dot(a, b):
    return jnp.dot(a, b, preferred_element_type=F32)


def _dot_nt(a, b):
    return lax.dot_general(a, b, (((1,), (1,)), ((), ())), preferred_element_type=F32)


def _dot_tn(a, b):
    return lax.dot_general(a, b, (((0,), (0,)), ((), ())), preferred_element_type=F32)


def _split_bf16(x):
    hi = x.astype(BF16)
    lo = (x - hi.astype(F32)).astype(BF16)
    return hi, lo


def _const_spec(shape):
    return pl.BlockSpec(shape, lambda *_: (0,) * len(shape))


def _layer_spec(shape, layer):
    return pl.BlockSpec((1,) + tuple(shape), lambda *_: (layer,) + (0,) * len(shape))


def _params(*semantics):
    return pltpu.CompilerParams(dimension_semantics=semantics, vmem_limit_bytes=VMEM_LIMIT)


def _ffn_body(x_ref, gain_ref, wgu_ref, wd_ref, o_ref, h_ref, acc_ref):
    ts = x_ref.shape[0] // FFN_SUB
    for i in range(FFN_SUB):
        x = x_ref[i * ts:(i + 1) * ts, :]
        h_ref[i] = _rms(x, gain_ref[0]).astype(BF16)
        acc_ref[i] = x
    for c in range(FFN_HIDDEN // FFN_CHUNK):
        lo = c * FFN_CHUNK
        for i in range(FFN_SUB):
            h = h_ref[i]
            g = _dot(h, wgu_ref[0, :, lo:lo + FFN_CHUNK])
            u = _dot(h, wgu_ref[0, :, FFN_HIDDEN + lo:FFN_HIDDEN + lo + FFN_CHUNK])
            a = (g / (1.0 + jnp.exp(-g)) * u).astype(BF16)
            acc_ref[i] += _dot(a, wd_ref[0, lo:lo + FFN_CHUNK, :])
    for i in range(FFN_SUB):
        o_ref[i * ts:(i + 1) * ts, :] = acc_ref[i]


def _ffn(x2, layer, ffn_norm, w_gate_up, w_down):
    m, d = x2.shape
    tm = min(FFN_TM, m)
    return pl.pallas_call(
        _ffn_body,
        out_shape=jax.ShapeDtypeStruct((m, d), F32),
        grid=(m // tm,),
        in_specs=[
            pl.BlockSpec((tm, d), lambda i: (i, 0)),
            _layer_spec((1, d), layer),
            _layer_spec((d, 2 * FFN_HIDDEN), layer),
            _layer_spec((FFN_HIDDEN, d), layer),
        ],
        out_specs=pl.BlockSpec((tm, d), lambda i: (i, 0)),
        scratch_shapes=[pltpu.VMEM((FFN_SUB, tm // FFN_SUB, d), BF16),
                        pltpu.VMEM((FFN_SUB, tm // FFN_SUB, d), F32)],
        compiler_params=_params("parallel"),
        name="ffn",
    )(x2, ffn_norm[:, None, :], w_gate_up, w_down)


def _xa_kv_body(mem_ref, gain_ref, wkv_ref, kgain_ref, k_ref, vt_ref):
    rows, mem = mem_ref.shape[0], mem_ref.shape[1]
    mh = jnp.concatenate([_rms(mem_ref[r], gain_ref[0]).astype(BF16) for r in range(rows)], axis=0)
    kv = _dot(mh, wkv_ref[0])
    for r in range(rows):
        kv_r = kv[r * mem:(r + 1) * mem]
        for h in range(XA_HEADS):
            lo = h * XA_HEAD_DIM
            k_ref[r, :, lo:lo + XA_HEAD_DIM] = _rms(kv_r[:, lo:lo + XA_HEAD_DIM], kgain_ref[0]).astype(BF16)
        vt_ref[r] = kv_r[:, D_MODEL:].T.astype(BF16)


def _xa_body(x_ref, gain_ref, wq_t_ref, qgain_ref, k_ref, vt_ref, wo_ref, o_ref, att_ref):
    hd = XA_HEAD_DIM
    mem = k_ref.shape[1]
    ts = x_ref.shape[1] // XA_SUB
    subs = range(XA_SUB)
    xs = [x_ref[0, i * ts:(i + 1) * ts, :] for i in subs]
    q_t = [_dot_nt(wq_t_ref[0], _rms(x, gain_ref[0]).astype(BF16)) for x in xs]
    qscale = hd ** -0.5 * float(np.log2(np.e))
    ones_rows = jnp.ones((2 * SUBLANES, mem), BF16)

    units = [(i, head) for i in subs for head in range(XA_HEADS)]
    scores = []
    for i, head in units:
        z = q_t[i][head * hd:(head + 1) * hd, :]
        ss = jnp.sum(z * z, axis=0, keepdims=True)
        z = z * (lax.rsqrt(ss * (1.0 / hd) + RMS_EPS) * qscale) * qgain_ref[0]
        scores.append(_dot(k_ref[0, :, head * hd:(head + 1) * hd], z.astype(BF16)))

    for i in subs:
        sub_units = [(u, unit) for u, unit in enumerate(units) if unit[0] == i]
        probs = [jnp.exp2(scores[u] - jnp.max(scores[u], axis=0, keepdims=True)).astype(BF16)
                 for u, _ in sub_units]
        for (u, (_, head)), p in zip(sub_units, probs):
            v_aug = jnp.concatenate([vt_ref[0, head * hd:(head + 1) * hd, :], ones_rows], axis=0)
            o_aug = _dot(v_aug, p)
            o = o_aug[0:hd, :] * (1.0 / o_aug[hd:hd + 1, :])
            att_ref[i, head * hd:(head + 1) * hd, :] = o.astype(BF16)
        o_ref[0, i * ts:(i + 1) * ts, :] = xs[i] + _dot_tn(att_ref[i], wo_ref[0])


def _cross_attention(x, mem, layer, mem_norm, xa_norm, wq_t, wkv, q_gain, k_gain, wo):
    b, t, d = x.shape
    m = mem.shape[1]
    k, vt = pl.pallas_call(
        _xa_kv_body,
        out_shape=(jax.ShapeDtypeStruct((b, m, d), BF16), jax.ShapeDtypeStruct((b, d, m), BF16)),
        grid=(b // XA_KV_ROWS,),
        in_specs=[
            pl.BlockSpec((XA_KV_ROWS, m, d), lambda i: (i, 0, 0)),
            _layer_spec((1, d), layer),
            _layer_spec((d, 2 * d), layer),
            _layer_spec((1, XA_HEAD_DIM), layer),
        ],
        out_specs=(pl.BlockSpec((XA_KV_ROWS, m, d), lambda i: (i, 0, 0)),
                   pl.BlockSpec((XA_KV_ROWS, d, m), lambda i: (i, 0, 0))),
        compiler_params=_params("parallel"),
        name="xa_kv",
    )(mem, mem_norm[:, None, :], wkv, k_gain[:, None, :])
    tm = min(XA_TM, t)
    return pl.pallas_call(
        _xa_body,
        out_shape=jax.ShapeDtypeStruct((b, t, d), F32),
        grid=(b, t // tm),
        in_specs=[
            pl.BlockSpec((1, tm, d), lambda i, j: (i, j, 0)),
            _layer_spec((1, d), layer),
            _layer_spec((d, d), layer),
            _layer_spec((XA_HEAD_DIM, 1), layer),
            pl.BlockSpec((1, m, d), lambda i, j: (i, 0, 0)),
            pl.BlockSpec((1, d, m), lambda i, j: (i, 0, 0)),
            _layer_spec((d, d), layer),
        ],
        out_specs=pl.BlockSpec((1, tm, d), lambda i, j: (i, j, 0)),
        scratch_shapes=[pltpu.VMEM((XA_SUB, d, tm // XA_SUB), BF16)],
        compiler_params=_params("parallel", "parallel"),
        name="xa",
    )(x, xa_norm[:, None, :], wq_t, q_gain[:, :, None], k, vt, wo)


_GLA_Q, _GLA_K, _GLA_V, _GLA_GO, _GLA_CB, _GLA_CC, _GLA_CI, _GLA_GL = (
    0, 256, 512, 1024, 1536, 2048, 2560, 3072)
GLA_PROJ_W = _GLA_GL + RANK_PAD


def _gla_body(x_ref, gain_ref, win_ref, wg2_ref, gbias_ref, ogain_ref, convw_ref, wout_ref,
              tri_ref, ones_ref, hmask_ref, smask_ref, o_ref,
              st_ref, ubuf_ref, utail_ref, cat_ref):
    rows, tm = x_ref.shape[0], x_ref.shape[1]
    seqs = range(rows)

    @pl.when(pl.program_id(1) == 0)
    def _():
        st_ref[...] = jnp.zeros_like(st_ref)
        utail_ref[...] = jnp.zeros_like(utail_ref)

    xs, projs, bs, b_lasts = [], [], [], []
    q_in, k_in, k_out, decay, v = [], [], [], [], []

    def project(r):
        xs.append(x_ref[r])
        projs.append(_dot(_rms(xs[r], gain_ref[...]).astype(BF16), win_ref[...]))

    def gates(r):
        logits = _dot(projs[r][:, _GLA_GL:_GLA_GL + RANK_PAD].astype(BF16), wg2_ref[...]) + gbias_ref[...]
        log_a = (jnp.minimum(logits, 0.0) - jnp.log(1.0 + jnp.exp(-jnp.abs(logits)))) * (1.0 / GLA_TAU)
        la_hi, la_lo = _split_bf16(log_a)
        bs.append(_dot(tri_ref[...], la_hi) + _dot(tri_ref[...], la_lo))
        b_lasts.append(_dot(ones_ref[...], la_hi) + _dot(ones_ref[...], la_lo))

    def vector_stage(r):
        proj, b, b_last = projs[r], bs[r], b_lasts[r]
        q = proj[:, _GLA_Q:_GLA_Q + GLA_KW] * (GLA_DK ** -0.5)
        k = proj[:, _GLA_K:_GLA_K + GLA_KW]
        v.append(proj[:, _GLA_V:_GLA_V + GLA_VW].astype(BF16))
        q_in.append((q * jnp.exp(b)).astype(BF16))
        k_in.append((k * jnp.exp(-b)).astype(BF16))
        k_out.append((k * jnp.exp(b_last - b)).astype(BF16))
        decay.append(jnp.exp(b_last))
        u = proj[:, _GLA_CC:_GLA_CC + CONV_DIM] * proj[:, _GLA_CI:_GLA_CI + CONV_DIM]
        ubuf_ref[r, 0:SUBLANES, :] = utail_ref[r]
        ubuf_ref[r, SUBLANES:SUBLANES + tm, :] = u
        y = (convw_ref[0:1, :] * ubuf_ref[r, SUBLANES - 2:SUBLANES - 2 + tm, :]
             + convw_ref[1:2, :] * ubuf_ref[r, SUBLANES - 1:SUBLANES - 1 + tm, :]
             + convw_ref[2:3, :] * u)
        cat_ref[r, :, GLA_VW:GLA_VW + CONV_DIM] = (proj[:, _GLA_CB:_GLA_CB + CONV_DIM] * y).astype(BF16)

    project(0)
    for r in seqs:
        if r + 1 < rows:
            project(r + 1)
        gates(r)
        vector_stage(r)

    o_inter = [[] for _ in seqs]
    for c in range(tm // GLA_CHUNK):
        r0 = c * GLA_CHUNK
        for r in seqs:
            st = st_ref[r]
            o_inter[r].append(_dot_nt(q_in[r][r0:r0 + GLA_CHUNK], st.astype(BF16)))
            kv_t = _dot_tn(v[r][r0:r0 + GLA_CHUNK], k_out[r][r0:r0 + GLA_CHUNK])
            st_ref[r] = st * decay[r][r0:r0 + 1, :] + jnp.where(smask_ref[...] > 0.5, kv_t, 0.0)
    o_inter = [jnp.concatenate(o, axis=0) for o in o_inter]

    causal = tri_ref[...] > 0.5
    for hd in range(GLA_HEADS):
        lo = hd * GLA_DV
        for r in seqs:
            q_h = jnp.where(hmask_ref[hd:hd + 1, :] > 0.5, q_in[r], jnp.zeros_like(q_in[r]))
            scores = jnp.where(causal, _dot_nt(q_h, k_in[r]), 0.0)
            o_h = _dot(scores.astype(BF16), v[r][:, lo:lo + GLA_DV]) + o_inter[r][:, lo:lo + GLA_DV]
            g_h = projs[r][:, _GLA_GO + lo:_GLA_GO + lo + GLA_DV]
            y_h = _rms(o_h, ogain_ref[...]) * (g_h / (1.0 + jnp.exp(-g_h)))
            cat_ref[r, :, lo:lo + GLA_DV] = y_h.astype(BF16)

    for r in seqs:
        o_ref[r] = xs[r] + _dot(cat_ref[r], wout_ref[...])

    @pl.when(pl.program_id(1) + 1 < pl.num_programs(1))
    def _():
        for r in seqs:
            utail_ref[r] = ubuf_ref[r, tm:tm + SUBLANES, :]


def _gla_layer(x, gain, w_in, w_gate2, gate_bias, out_gain, conv_w, w_out):
    b, t, d = x.shape
    tm = min(GLA_TM, t)
    rows = GLA_ROWS
    widths = [GLA_KW, GLA_KW, GLA_VW, GLA_RANK, GLA_VW, CONV_DIM, CONV_DIM, CONV_DIM]
    cuts = [int(c) for c in np.cumsum(widths)[:-1]]
    wq, wk, wv, wgl, wgo, wcb, wcc, wci = jnp.split(w_in, cuts, axis=1)
    wgl = jnp.pad(wgl, ((0, 0), (0, RANK_PAD - GLA_RANK)))
    w_in_r = jnp.concatenate([wq, wk, wv, wgo, wcb, wcc, wci, wgl], axis=1).astype(BF16)
    wg2 = jnp.pad(w_gate2, ((0, RANK_PAD - GLA_RANK), (0, 0))).astype(BF16)

    r = np.arange(tm)
    same_chunk = (r[:, None] // GLA_CHUNK) == (r[None, :] // GLA_CHUNK)
    tri = jnp.asarray(same_chunk & (r[None, :] <= r[:, None]), BF16)
    ones = jnp.asarray(same_chunk, BF16)
    hmask = jnp.asarray(np.arange(GLA_KW)[None, :] // GLA_DK == np.arange(GLA_HEADS)[:, None], F32)
    smask = jnp.asarray(
        (np.arange(GLA_VW)[:, None] // GLA_DV) == (np.arange(GLA_KW)[None, :] // GLA_DK), F32)

    return pl.pallas_call(
        _gla_body,
        out_shape=jax.ShapeDtypeStruct((b, t, d), F32),
        grid=(b // rows, t // tm),
        in_specs=[
            pl.BlockSpec((rows, tm, d), lambda i, j: (i, j, 0)),
            _const_spec((1, d)),
            _const_spec((d, GLA_PROJ_W)),
            _const_spec((RANK_PAD, GLA_KW)),
            _const_spec((1, GLA_KW)),
            _const_spec((1, GLA_DV)),
            _const_spec((CONV_WIDTH, CONV_DIM)),
            _const_spec((GLA_VW + CONV_DIM, d)),
            _const_spec((tm, tm)),
            _const_spec((tm, tm)),
            _const_spec((GLA_HEADS, GLA_KW)),
            _const_spec((GLA_VW, GLA_KW)),
        ],
        out_specs=pl.BlockSpec((rows, tm, d), lambda i, j: (i, j, 0)),
        scratch_shapes=[
            pltpu.VMEM((rows, GLA_VW, GLA_KW), F32),
            pltpu.VMEM((rows, tm + SUBLANES, CONV_DIM), F32),
            pltpu.VMEM((rows, SUBLANES, CONV_DIM), F32),
            pltpu.VMEM((rows, tm, GLA_VW + CONV_DIM), BF16),
        ],
        compiler_params=_params("parallel", "arbitrary"),
        name="gla_conv",
    )(x, gain.reshape(1, d), w_in_r, wg2, gate_bias.reshape(1, GLA_KW), out_gain.reshape(1, GLA_DV),
      conv_w, w_out.astype(BF16), tri, ones, hmask, smask)


def _norm_rope_heads(z, n_heads, gain, cos_f, sin_f):
    hd, half = SWA_HEAD_DIM, ROT_DIM // 2
    out = []
    for head in range(n_heads):
        zh = z[head * hd:(head + 1) * hd, :]
        ss = jnp.sum(zh * zh, axis=0, keepdims=True)
        zh = zh * lax.rsqrt(ss * (1.0 / hd) + RMS_EPS) * gain
        x1, x2 = zh[0:half], zh[half:ROT_DIM]
        out.append(jnp.concatenate(
            [x1 * cos_f - x2 * sin_f, x2 * cos_f + x1 * sin_f, zh[ROT_DIM:]], axis=0))
    return out


def _swa_body(x_ref, gain_ref, wqkv_t_ref, qgain_ref, kgain_ref, cos_ref, sin_ref,
              band_ref, band0_ref, sink_ref, wout_ref, o_ref,
              qpad_ref, kbuf_ref, vbuf_ref, kprev_ref, vprev_ref, att_ref):
    n_seq, tq = x_ref.shape[0], x_ref.shape[1]
    seqs = range(n_seq)
    nb = tq // WINDOW
    n = pl.program_id(1)
    hd = SWA_HEAD_DIM
    qw = SWA_HEADS * hd
    kw = SWA_KV_HEADS * hd

    @pl.when((pl.program_id(0) == 0) & (n == 0))
    def _():
        qpad_ref[...] = jnp.zeros_like(qpad_ref)

    @pl.when(n == 0)
    def _():
        kprev_ref[...] = jnp.zeros_like(kprev_ref)
        vprev_ref[...] = jnp.zeros_like(vprev_ref)

    xs = [x_ref[r] for r in seqs]
    qkv_t = [_dot_nt(wqkv_t_ref[...], _rms(x, gain_ref[...]).astype(BF16)) for x in xs]

    for r in seqs:
        cos_f, sin_f = cos_ref[r], sin_ref[r]
        k_heads = _norm_rope_heads(qkv_t[r][qw:qw + kw], SWA_KV_HEADS, kgain_ref[...], cos_f, sin_f)
        kbuf_ref[r, 0:WINDOW, :] = kprev_ref[r]
        kbuf_ref[r, WINDOW:WINDOW + tq, :] = jnp.concatenate(k_heads, axis=0).T.astype(BF16)
        vbuf_ref[r, :, 0:WINDOW] = vprev_ref[r]
        vbuf_ref[r, :, WINDOW:WINDOW + tq] = qkv_t[r][qw + kw:, :].astype(BF16)
        q_heads = _norm_rope_heads(qkv_t[r][:qw], SWA_HEADS, qgain_ref[...], cos_f, sin_f)
        for head, z in enumerate(q_heads):
            g, j = divmod(head, SWA_GROUP)
            z = z.astype(BF16)
            for i in range(nb):
                qpad_ref[r, g, i, g * hd:(g + 1) * hd, j * WINDOW:(j + 1) * WINDOW] = (
                    z[:, i * WINDOW:(i + 1) * WINDOW])

    bias0 = jnp.where(n > 0, band_ref[...], band0_ref[...])

    units = [(r, g, i) for g in range(SWA_KV_HEADS) for i in range(nb) for r in seqs]

    def scores(unit):
        r, g, i = unit
        return _dot(kbuf_ref[r, i * WINDOW:(i + 2) * WINDOW, :], qpad_ref[r, g, i])

    ones_rows = jnp.ones((2 * SUBLANES, 2 * WINDOW), BF16)
    waves = [units[w:w + SWA_WAVE] for w in range(0, len(units), SWA_WAVE)]
    pending = [[scores(unit) for unit in wave] for wave in waves[:SWA_LOOKAHEAD]]
    for w, wave in enumerate(waves):
        s_wave = pending.pop(0)
        if w + SWA_LOOKAHEAD < len(waves):
            pending.append([scores(unit) for unit in waves[w + SWA_LOOKAHEAD]])
        stats, probs = [], []
        for (r, g, i), s in zip(wave, s_wave):
            s = s + (bias0 if i == 0 else band_ref[...])
            sink = sink_ref[g]
            m = jnp.maximum(jnp.max(s, axis=0, keepdims=True), sink)
            if w + SWA_LOOKAHEAD < len(waves):
                m = m + jnp.minimum(jnp.abs(pending[-1][0][0:1, :]), 0.0)
            probs.append(jnp.exp2(s - m).astype(BF16))
            stats.append((sink, m))
        for (r, g, i), p, (sink, m) in zip(wave, probs, stats):
            r0 = i * WINDOW
            v_aug = jnp.concatenate(
                [vbuf_ref[r, g * hd:(g + 1) * hd, r0:r0 + 2 * WINDOW], ones_rows], axis=0)
            o_aug = _dot(v_aug, p)
            denom = o_aug[hd:hd + 1, :] + jnp.exp2(sink - m)
            o = o_aug[0:hd, :] * (1.0 / denom)
            for j in range(SWA_GROUP):
                head = g * SWA_GROUP + j
                att_ref[r, head * hd:(head + 1) * hd, r0:r0 + WINDOW] = (
                    o[:, j * WINDOW:(j + 1) * WINDOW].astype(BF16))

    for r in seqs:
        o_ref[r] = xs[r] + _dot_tn(att_ref[r], wout_ref[...])

    @pl.when(n + 1 < pl.num_programs(1))
    def _():
        for r in seqs:
            kprev_ref[r] = kbuf_ref[r, tq:tq + WINDOW, :]
            vprev_ref[r] = vbuf_ref[r, :, tq:tq + WINDOW]


def _swa_layer(x, positions, gain, w_qkv, q_gain, k_gain, sinks, w_out):
    b, t, d = x.shape
    tq = min(SWA_TQ, t)
    nb = tq // WINDOW
    hd = SWA_HEAD_DIM
    qw = SWA_HEADS * hd
    kw = SWA_KV_HEADS * hd
    inv_freq = ROPE_THETA ** (-jnp.arange(0, ROT_DIM, 2, dtype=F32) / ROT_DIM)
    ang = positions.astype(F32)[:, None, :] * inv_freq[None, :, None]
    cos_f, sin_f = jnp.cos(ang), jnp.sin(ang)
    log2e = float(np.log2(np.e))
    qgain_f = jnp.broadcast_to((q_gain * (hd ** -0.5 * log2e))[:, None], (hd, tq))
    kgain_f = jnp.broadcast_to(k_gain[:, None], (hd, tq))
    rows = SWA_GROUP * WINDOW
    si = np.arange(2 * WINDOW)[:, None]
    qi = np.arange(WINDOW)[None, :] + WINDOW
    band_ok = (qi - si >= 0) & (qi - si < WINDOW)
    band = jnp.asarray(np.tile(np.where(band_ok, 0.0, -np.inf), (1, SWA_GROUP)), F32)
    band0 = jnp.asarray(np.tile(np.where(band_ok & (si >= WINDOW), 0.0, -np.inf), (1, SWA_GROUP)), F32)
    sink_cols = jnp.repeat((sinks.astype(F32) * log2e).reshape(SWA_KV_HEADS, 1, SWA_GROUP), WINDOW, axis=2)

    n_seq = SWA_ROWS
    row_spec = pl.BlockSpec((n_seq, tq, d), lambda i, j: (i, j, 0))
    tab_spec = pl.BlockSpec((n_seq, ROT_DIM // 2, tq), lambda i, j: (i, 0, j))
    return pl.pallas_call(
        _swa_body,
        out_shape=jax.ShapeDtypeStruct((b, t, d), F32),
        grid=(b // n_seq, t // tq),
        in_specs=[
            row_spec,
            _const_spec((1, d)),
            _const_spec((qw + 2 * kw, d)),
            _const_spec((hd, tq)),
            _const_spec((hd, tq)),
            tab_spec, tab_spec,
            _const_spec((2 * WINDOW, rows)),
            _const_spec((2 * WINDOW, rows)),
            _const_spec((SWA_KV_HEADS, 1, rows)),
            _const_spec((d, d)),
        ],
        out_specs=row_spec,
        scratch_shapes=[
            pltpu.VMEM((n_seq, SWA_KV_HEADS, nb, kw, rows), BF16),
            pltpu.VMEM((n_seq, WINDOW + tq, kw), BF16),
            pltpu.VMEM((n_seq, kw, WINDOW + tq), BF16),
            pltpu.VMEM((n_seq, WINDOW, kw), BF16),
            pltpu.VMEM((n_seq, kw, WINDOW), BF16),
            pltpu.VMEM((n_seq, qw, tq), BF16),
        ],
        compiler_params=_params("arbitrary", "arbitrary"),
        name="swa",
    )(x, gain.reshape(1, d), w_qkv.T.astype(BF16), qgain_f, kgain_f, cos_f, sin_f,
      band, band0, sink_cols, w_out.astype(BF16))


def kernel(x, mem, positions, mix_norm, hyb_w_in, gla_w_gate2, gla_gate_bias, gla_out_gain, conv_w,
           hyb_w_out, swa_w_qkv, swa_q_gain, swa_k_gain, swa_sinks, swa_w_out, mem_norm, xa_norm, xa_wq,
           xa_wkv, xa_q_gain, xa_k_gain, xa_wo, ffn_norm, ffn_w_gate_up, ffn_w_down):
    b, t, d = x.shape
    depth = mix_norm.shape[0]
    xa_wq_t = jnp.swapaxes(xa_wq, 1, 2).astype(BF16)
    xa_wkv_b, xa_wo_b = xa_wkv.astype(BF16), xa_wo.astype(BF16)
    ffn_wgu_b, ffn_wd_b = ffn_w_gate_up.astype(BF16), ffn_w_down.astype(BF16)
    for layer in range(depth):
        i = layer // 2
        if layer % 2 == 0:
            x = _gla_layer(x, mix_norm[layer], hyb_w_in[i], gla_w_gate2[i], gla_gate_bias[i],
                           gla_out_gain[i], conv_w[i], hyb_w_out[i])
        else:
            x = _swa_layer(x, positions, mix_norm[layer], swa_w_qkv[i], swa_q_gain[i], swa_k_gain[i],
                           swa_sinks[i], swa_w_out[i])
        x = _cross_attention(x, mem, layer, mem_norm, xa_norm, xa_wq_t, xa_wkv_b, xa_q_gain, xa_k_gain,
                             xa_wo_b)
        x = _ffn(x.reshape(b * t, d), layer, ffn_norm, ffn_wgu_b, ffn_wd_b).reshape(b, t, d)
    return x
```

```python
import numpy as np
import jax
import jax.numpy as jnp
from jax import lax
from jax.experimental import pallas as pl
from jax.experimental.pallas import tpu as pltpu

F32 = jnp.float32
BF16 = jnp.bfloat16

D_MODEL = 1024
RMS_EPS = 1e-6
GLA_HEADS = 4
GLA_DK = 64
GLA_DV = 128
GLA_RANK = 16
GLA_TAU = 16.0
GLA_CHUNK = 64
GLA_KW = GLA_HEADS * GLA_DK
GLA_VW = GLA_HEADS * GLA_DV
CONV_DIM = 512
CONV_WIDTH = 3
RANK_PAD = 128
SWA_HEADS = 16
SWA_KV_HEADS = 4
SWA_GROUP = SWA_HEADS // SWA_KV_HEADS
SWA_HEAD_DIM = 64
WINDOW = 128
ROT_DIM = 16
ROPE_THETA = 500000.0
XA_HEADS = 4
XA_HEAD_DIM = 256
FFN_HIDDEN = 2816
FFN_CHUNK = 256

SUBLANES = 8
VMEM_LIMIT = 56 * 1024 * 1024

GLA_TM = 256
GLA_ROWS = 4
SWA_TQ = 512
SWA_ROWS = 2
SWA_WAVE = 8
SWA_LOOKAHEAD = 1
XA_TM = 1024
XA_KV_ROWS = 2
XA_SUB = 2
FFN_TM = 1024
FFN_SUB = 2


def _rms(x, gain):
    return x * lax.rsqrt(jnp.mean(x * x, axis=-1, keepdims=True) + RMS_EPS) * gain


def _dot(a, b):
    return jnp.dot(a, b, preferred_element_type=F32)


def _dot_nt(a, b):
    return lax.dot_general(a, b, (((1,), (1,)), ((), ())), preferred_element_type=F32)


def _dot_tn(a, b):
    return lax.dot_general(a, b, (((0,), (0,)), ((), ())), preferred_element_type=F32)


def _split_bf16(x):
    hi = x.astype(BF16)
    lo = (x - hi.astype(F32)).astype(BF16)
    return hi, lo


def _const_spec(shape):
    return pl.BlockSpec(shape, lambda *_: (0,) * len(shape))


def _layer_spec(shape, layer):
    return pl.BlockSpec((1,) + tuple(shape), lambda *_: (layer,) + (0,) * len(shape))


def _params(*semantics):
    return pltpu.CompilerParams(dimension_semantics=semantics, vmem_limit_bytes=VMEM_LIMIT)


def _ffn_body(x_ref, gain_ref, wgu_ref, wd_ref, o_ref, h_ref, acc_ref):
    ts = x_ref.shape[0] // FFN_SUB
    for i in range(FFN_SUB):
        x = x_ref[i * ts:(i + 1) * ts, :]
        h_ref[i] = _rms(x, gain_ref[0]).astype(BF16)
        acc_ref[i] = x
    for c in range(FFN_HIDDEN // FFN_CHUNK):
        lo = c * FFN_CHUNK
        for i in range(FFN_SUB):
            h = h_ref[i]
            g = _dot(h, wgu_ref[0, :, lo:lo + FFN_CHUNK])
            u = _dot(h, wgu_ref[0, :, FFN_HIDDEN + lo:FFN_HIDDEN + lo + FFN_CHUNK])
            a = (g / (1.0 + jnp.exp(-g)) * u).astype(BF16)
            acc_ref[i] += _dot(a, wd_ref[0, lo:lo + FFN_CHUNK, :])
    for i in range(FFN_SUB):
        o_ref[i * ts:(i + 1) * ts, :] = acc_ref[i]


def _ffn(x2, layer, ffn_norm, w_gate_up, w_down):
    m, d = x2.shape
    tm = min(FFN_TM, m)
    return pl.pallas_call(
        _ffn_body,
        out_shape=jax.ShapeDtypeStruct((m, d), F32),
        grid=(m // tm,),
        in_specs=[
            pl.BlockSpec((tm, d), lambda i: (i, 0)),
            _layer_spec((1, d), layer),
            _layer_spec((d, 2 * FFN_HIDDEN), layer),
            _layer_spec((FFN_HIDDEN, d), layer),
        ],
        out_specs=pl.BlockSpec((tm, d), lambda i: (i, 0)),
        scratch_shapes=[pltpu.VMEM((FFN_SUB, tm // FFN_SUB, d), BF16),
                        pltpu.VMEM((FFN_SUB, tm // FFN_SUB, d), F32)],
        compiler_params=_params("parallel"),
        name="ffn",
    )(x2, ffn_norm[:, None, :], w_gate_up, w_down)


def _xa_kv_body(mem_ref, gain_ref, wkv_ref, kgain_ref, k_ref, vt_ref):
    rows, mem = mem_ref.shape[0], mem_ref.shape[1]
    mh = jnp.concatenate([_rms(mem_ref[r], gain_ref[0]).astype(BF16) for r in range(rows)], axis=0)
    kv = _dot(mh, wkv_ref[0])
    for r in range(rows):
        kv_r = kv[r * mem:(r + 1) * mem]
        for h in range(XA_HEADS):
            lo = h * XA_HEAD_DIM
            k_ref[r, :, lo:lo + XA_HEAD_DIM] = _rms(kv_r[:, lo:lo + XA_HEAD_DIM], kgain_ref[0]).astype(BF16)
        vt_ref[r] = kv_r[:, D_MODEL:].T.astype(BF16)


def _xa_body(x_ref, gain_ref, wq_t_ref, qgain_ref, k_ref, vt_ref, wo_ref, o_ref, att_ref):
    hd = XA_HEAD_DIM
    mem = k_ref.shape[1]
    ts = x_ref.shape[1] // XA_SUB
    qscale = hd ** -0.5 * float(np.log2(np.e))
    ones_rows = jnp.ones((2 * SUBLANES, mem), BF16)
    xs, q_t, scores = {}, {}, {}

    def project(i):
        xs[i] = x_ref[0, i * ts:(i + 1) * ts, :]
        q_t[i] = _dot_nt(wq_t_ref[0], _rms(xs[i], gain_ref[0]).astype(BF16))

    def score(i):
        scores[i] = []
        for head in range(XA_HEADS):
            z = q_t[i][head * hd:(head + 1) * hd, :]
            ss = jnp.sum(z * z, axis=0, keepdims=True)
            z = z * (lax.rsqrt(ss * (1.0 / hd) + RMS_EPS) * qscale) * qgain_ref[0]
            scores[i].append(_dot(k_ref[0, :, head * hd:(head + 1) * hd], z.astype(BF16)))

    def attend(i):
        probs = [jnp.exp2(s - jnp.max(s, axis=0, keepdims=True)).astype(BF16) for s in scores[i]]
        for head, p in enumerate(probs):
            v_aug = jnp.concatenate([vt_ref[0, head * hd:(head + 1) * hd, :], ones_rows], axis=0)
            o_aug = _dot(v_aug, p)
            o = o_aug[0:hd, :] * (1.0 / o_aug[hd:hd + 1, :])
            att_ref[i, head * hd:(head + 1) * hd, :] = o.astype(BF16)

    def project_out(i):
        o_ref[0, i * ts:(i + 1) * ts, :] = xs[i] + _dot_tn(att_ref[i], wo_ref[0])

    project(0)
    score(0)
    for i in range(1, XA_SUB):
        project(i)
        attend(i - 1)
        score(i)
        project_out(i - 1)
    attend(XA_SUB - 1)
    project_out(XA_SUB - 1)


def _cross_attention(x, mem, layer, mem_norm, xa_norm, wq_t, wkv, q_gain, k_gain, wo):
    b, t, d = x.shape
    m = mem.shape[1]
    k, vt = pl.pallas_call(
        _xa_kv_body,
        out_shape=(jax.ShapeDtypeStruct((b, m, d), BF16), jax.ShapeDtypeStruct((b, d, m), BF16)),
        grid=(b // XA_KV_ROWS,),
        in_specs=[
            pl.BlockSpec((XA_KV_ROWS, m, d), lambda i: (i, 0, 0)),
            _layer_spec((1, d), layer),
            _layer_spec((d, 2 * d), layer),
            _layer_spec((1, XA_HEAD_DIM), layer),
        ],
        out_specs=(pl.BlockSpec((XA_KV_ROWS, m, d), lambda i: (i, 0, 0)),
                   pl.BlockSpec((XA_KV_ROWS, d, m), lambda i: (i, 0, 0))),
        compiler_params=_params("parallel"),
        name="xa_kv",
    )(mem, mem_norm[:, None, :], wkv, k_gain[:, None, :])
    tm = min(XA_TM, t)
    return pl.pallas_call(
        _xa_body,
        out_shape=jax.ShapeDtypeStruct((b, t, d), F32),
        grid=(b, t // tm),
        in_specs=[
            pl.BlockSpec((1, tm, d), lambda i, j: (i, j, 0)),
            _layer_spec((1, d), layer),
            _layer_spec((d, d), layer),
            _layer_spec((XA_HEAD_DIM, 1), layer),
            pl.BlockSpec((1, m, d), lambda i, j: (i, 0, 0)),
            pl.BlockSpec((1, d, m), lambda i, j: (i, 0, 0)),
            _layer_spec((d, d), layer),
        ],
        out_specs=pl.BlockSpec((1, tm, d), lambda i, j: (i, j, 0)),
        scratch_shapes=[pltpu.VMEM((XA_SUB, d, tm // XA_SUB), BF16)],
        compiler_params=_params("parallel", "parallel"),
        name="xa",
    )(x, xa_norm[:, None, :], wq_t, q_gain[:, :, None], k, vt, wo)


_GLA_Q, _GLA_K, _GLA_V, _GLA_GO, _GLA_CB, _GLA_CC, _GLA_CI, _GLA_GL = (
    0, 256, 512, 1024, 1536, 2048, 2560, 3072)
GLA_PROJ_W = _GLA_GL + RANK_PAD


def _gla_body(x_ref, gain_ref, win_ref, wg2_ref, gbias_ref, ogain_ref, convw_ref, wout_ref,
              tri_ref, ones_ref, hmask_ref, smask_ref, o_ref,
              st_ref, ubuf_ref, utail_ref, cat_ref):
    rows, tm = x_ref.shape[0], x_ref.shape[1]
    seqs = range(rows)

    @pl.when(pl.program_id(1) == 0)
    def _():
        st_ref[...] = jnp.zeros_like(st_ref)
        utail_ref[...] = jnp.zeros_like(utail_ref)

    xs, projs, bs, b_lasts = [], [], [], []
    q_in, k_in, k_out, decay, v = [], [], [], [], []

    def project(r):
        xs.append(x_ref[r])
        projs.append(_dot(_rms(xs[r], gain_ref[...]).astype(BF16), win_ref[...]))

    def gates(r):
        logits = _dot(projs[r][:, _GLA_GL:_GLA_GL + RANK_PAD].astype(BF16), wg2_ref[...]) + gbias_ref[...]
        log_a = (jnp.minimum(logits, 0.0) - jnp.log(1.0 + jnp.exp(-jnp.abs(logits)))) * (1.0 / GLA_TAU)
        la_hi, la_lo = _split_bf16(log_a)
        bs.append(_dot(tri_ref[...], la_hi) + _dot(tri_ref[...], la_lo))
        b_lasts.append(_dot(ones_ref[...], la_hi) + _dot(ones_ref[...], la_lo))

    def vector_stage(r):
        proj, b, b_last = projs[r], bs[r], b_lasts[r]
        q = proj[:, _GLA_Q:_GLA_Q + GLA_KW] * (GLA_DK ** -0.5)
        k = proj[:, _GLA_K:_GLA_K + GLA_KW]
        v.append(proj[:, _GLA_V:_GLA_V + GLA_VW].astype(BF16))
        q_in.append((q * jnp.exp(b)).astype(BF16))
        k_in.append((k * jnp.exp(-b)).astype(BF16))
        k_out.append((k * jnp.exp(b_last - b)).astype(BF16))
        decay.append(jnp.exp(b_last))
        u = proj[:, _GLA_CC:_GLA_CC + CONV_DIM] * proj[:, _GLA_CI:_GLA_CI + CONV_DIM]
        ubuf_ref[r, 0:SUBLANES, :] = utail_ref[r]
        ubuf_ref[r, SUBLANES:SUBLANES + tm, :] = u
        y = (convw_ref[0:1, :] * ubuf_ref[r, SUBLANES - 2:SUBLANES - 2 + tm, :]
             + convw_ref[1:2, :] * ubuf_ref[r, SUBLANES - 1:SUBLANES - 1 + tm, :]
             + convw_ref[2:3, :] * u)
        cat_ref[r, :, GLA_VW:GLA_VW + CONV_DIM] = (proj[:, _GLA_CB:_GLA_CB + CONV_DIM] * y).astype(BF16)

    project(0)
    for r in seqs:
        if r + 1 < rows:
            project(r + 1)
        gates(r)
        vector_stage(r)

    o_inter = [[] for _ in seqs]
    for c in range(tm // GLA_CHUNK):
        r0 = c * GLA_CHUNK
        for r in seqs:
            st = st_ref[r]
            o_inter[r].append(_dot_nt(q_in[r][r0:r0 + GLA_CHUNK], st.astype(BF16)))
            kv_t = _dot_tn(v[r][r0:r0 + GLA_CHUNK], k_out[r][r0:r0 + GLA_CHUNK])
            st_ref[r] = st * decay[r][r0:r0 + 1, :] + jnp.where(smask_ref[...] > 0.5, kv_t, 0.0)
    o_inter = [jnp.concatenate(o, axis=0) for o in o_inter]

    causal = tri_ref[...] > 0.5
    for hd in range(GLA_HEADS):
        lo = hd * GLA_DV
        for r in seqs:
            q_h = jnp.where(hmask_ref[hd:hd + 1, :] > 0.5, q_in[r], jnp.zeros_like(q_in[r]))
            scores = jnp.where(causal, _dot_nt(q_h, k_in[r]), 0.0)
            o_h = _dot(scores.astype(BF16), v[r][:, lo:lo + GLA_DV]) + o_inter[r][:, lo:lo + GLA_DV]
            g_h = projs[r][:, _GLA_GO + lo:_GLA_GO + lo + GLA_DV]
            y_h = _rms(o_h, ogain_ref[...]) * (g_h / (1.0 + jnp.exp(-g_h)))
            cat_ref[r, :, lo:lo + GLA_DV] = y_h.astype(BF16)

    for r in seqs:
        o_ref[r] = xs[r] + _dot(cat_ref[r], wout_ref[...])

    @pl.when(pl.program_id(1) + 1 < pl.num_programs(1))
    def _():
        for r in seqs:
            utail_ref[r] = ubuf_ref[r, tm:tm + SUBLANES, :]


def _gla_layer(x, gain, w_in, w_gate2, gate_bias, out_gain, conv_w, w_out):
    b, t, d = x.shape
    tm = min(GLA_TM, t)
    rows = GLA_ROWS
    widths = [GLA_KW, GLA_KW, GLA_VW, GLA_RANK, GLA_VW, CONV_DIM, CONV_DIM, CONV_DIM]
    cuts = [int(c) for c in np.cumsum(widths)[:-1]]
    wq, wk, wv, wgl, wgo, wcb, wcc, wci = jnp.split(w_in, cuts, axis=1)
    wgl = jnp.pad(wgl, ((0, 0), (0, RANK_PAD - GLA_RANK)))
    w_in_r = jnp.concatenate([wq, wk, wv, wgo, wcb, wcc, wci, wgl], axis=1).astype(BF16)
    wg2 = jnp.pad(w_gate2, ((0, RANK_PAD - GLA_RANK), (0, 0))).astype(BF16)

    r = np.arange(tm)
    same_chunk = (r[:, None] // GLA_CHUNK) == (r[None, :] // GLA_CHUNK)
    tri = jnp.asarray(same_chunk & (r[None, :] <= r[:, None]), BF16)
    ones = jnp.asarray(same_chunk, BF16)
    hmask = jnp.asarray(np.arange(GLA_KW)[None, :] // GLA_DK == np.arange(GLA_HEADS)[:, None], F32)
    smask = jnp.asarray(
        (np.arange(GLA_VW)[:, None] // GLA_DV) == (np.arange(GLA_KW)[None, :] // GLA_DK), F32)

    return pl.pallas_call(
        _gla_body,
        out_shape=jax.ShapeDtypeStruct((b, t, d), F32),
        grid=(b // rows, t // tm),
        in_specs=[
            pl.BlockSpec((rows, tm, d), lambda i, j: (i, j, 0)),
            _const_spec((1, d)),
            _const_spec((d, GLA_PROJ_W)),
            _const_spec((RANK_PAD, GLA_KW)),
            _const_spec((1, GLA_KW)),
            _const_spec((1, GLA_DV)),
            _const_spec((CONV_WIDTH, CONV_DIM)),
            _const_spec((GLA_VW + CONV_DIM, d)),
            _const_spec((tm, tm)),
            _const_spec((tm, tm)),
            _const_spec((GLA_HEADS, GLA_KW)),
            _const_spec((GLA_VW, GLA_KW)),
        ],
        out_specs=pl.BlockSpec((rows, tm, d), lambda i, j: (i, j, 0)),
        scratch_shapes=[
            pltpu.VMEM((rows, GLA_VW, GLA_KW), F32),
            pltpu.VMEM((rows, tm + SUBLANES, CONV_DIM), F32),
            pltpu.VMEM((rows, SUBLANES, CONV_DIM), F32),
            pltpu.VMEM((rows, tm, GLA_VW + CONV_DIM), BF16),
        ],
        compiler_params=_params("parallel", "arbitrary"),
        name="gla_conv",
    )(x, gain.reshape(1, d), w_in_r, wg2, gate_bias.reshape(1, GLA_KW), out_gain.reshape(1, GLA_DV),
      conv_w, w_out.astype(BF16), tri, ones, hmask, smask)


def _norm_rope_heads(z, n_heads, gain, cos_f, sin_f):
    hd, half = SWA_HEAD_DIM, ROT_DIM // 2
    out = []
    for head in range(n_heads):
        zh = z[head * hd:(head + 1) * hd, :]
        ss = jnp.sum(zh * zh, axis=0, keepdims=True)
        zh = zh * lax.rsqrt(ss * (1.0 / hd) + RMS_EPS) * gain
        x1, x2 = zh[0:half], zh[half:ROT_DIM]
        out.append(jnp.concatenate(
            [x1 * cos_f - x2 * sin_f, x2 * cos_f + x1 * sin_f, zh[ROT_DIM:]], axis=0))
    return out


def _swa_body(x_ref, gain_ref, wqkv_t_ref, qgain_ref, kgain_ref, cos_ref, sin_ref,
              band_ref, band0_ref, sink_ref, wout_ref, o_ref,
              qpad_ref, kbuf_ref, vbuf_ref, kprev_ref, vprev_ref, att_ref):
    n_seq, tq = x_ref.shape[0], x_ref.shape[1]
    seqs = range(n_seq)
    nb = tq // WINDOW
    n = pl.program_id(1)
    hd = SWA_HEAD_DIM
    qw = SWA_HEADS * hd
    kw = SWA_KV_HEADS * hd

    @pl.when((pl.program_id(0) == 0) & (n == 0))
    def _():
        qpad_ref[...] = jnp.zeros_like(qpad_ref)

    @pl.when(n == 0)
    def _():
        kprev_ref[...] = jnp.zeros_like(kprev_ref)
        vprev_ref[...] = jnp.zeros_like(vprev_ref)

    xs = [x_ref[r] for r in seqs]
    qkv_t = [_dot_nt(wqkv_t_ref[...], _rms(x, gain_ref[...]).astype(BF16)) for x in xs]

    for r in seqs:
        cos_f, sin_f = cos_ref[r], sin_ref[r]
        k_heads = _norm_rope_heads(qkv_t[r][qw:qw + kw], SWA_KV_HEADS, kgain_ref[...], cos_f, sin_f)
        kbuf_ref[r, 0:WINDOW, :] = kprev_ref[r]
        kbuf_ref[r, WINDOW:WINDOW + tq, :] = jnp.concatenate(k_heads, axis=0).T.astype(BF16)
        vbuf_ref[r, :, 0:WINDOW] = vprev_ref[r]
        vbuf_ref[r, :, WINDOW:WINDOW + tq] = qkv_t[r][qw + kw:, :].astype(BF16)
        q_heads = _norm_rope_heads(qkv_t[r][:qw], SWA_HEADS, qgain_ref[...], cos_f, sin_f)
        for head, z in enumerate(q_heads):
            g, j = divmod(head, SWA_GROUP)
            z = z.astype(BF16)
            for i in range(nb):
                qpad_ref[r, g, i, g * hd:(g + 1) * hd, j * WINDOW:(j + 1) * WINDOW] = (
                    z[:, i * WINDOW:(i + 1) * WINDOW])

    bias0 = jnp.where(n > 0, band_ref[...], band0_ref[...])

    units = [(r, g, i) for g in range(SWA_KV_HEADS) for i in range(nb) for r in seqs]

    def scores(unit):
        r, g, i = unit
        return _dot(kbuf_ref[r, i * WINDOW:(i + 2) * WINDOW, :], qpad_ref[r, g, i])

    ones_rows = jnp.ones((2 * SUBLANES, 2 * WINDOW), BF16)
    waves = [units[w:w + SWA_WAVE] for w in range(0, len(units), SWA_WAVE)]
    pending = [[scores(unit) for unit in wave] for wave in waves[:SWA_LOOKAHEAD]]
    for w, wave in enumerate(waves):
        s_wave = pending.pop(0)
        if w + SWA_LOOKAHEAD < len(waves):
            pending.append([scores(unit) for unit in waves[w + SWA_LOOKAHEAD]])
        stats, probs = [], []
        for (r, g, i), s in zip(wave, s_wave):
            s = s + (bias0 if i == 0 else band_ref[...])
            sink = sink_ref[g]
            m = jnp.maximum(jnp.max(s, axis=0, keepdims=True), sink)
            if w + SWA_LOOKAHEAD < len(waves):
                m = m + jnp.minimum(jnp.abs(pending[-1][0][0:1, :]), 0.0)
            probs.append(jnp.exp2(s - m).astype(BF16))
            stats.append((sink, m))
        for (r, g, i), p, (sink, m) in zip(wave, probs, stats):
            r0 = i * WINDOW
            v_aug = jnp.concatenate(
                [vbuf_ref[r, g * hd:(g + 1) * hd, r0:r0 + 2 * WINDOW], ones_rows], axis=0)
            o_aug = _dot(v_aug, p)
            denom = o_aug[hd:hd + 1, :] + jnp.exp2(sink - m)
            o = o_aug[0:hd, :] * (1.0 / denom)
            for j in range(SWA_GROUP):
                head = g * SWA_GROUP + j
                att_ref[r, head * hd:(head + 1) * hd, r0:r0 + WINDOW] = (
                    o[:, j * WINDOW:(j + 1) * WINDOW].astype(BF16))

    for r in seqs:
        o_ref[r] = xs[r] + _dot_tn(att_ref[r], wout_ref[...])

    @pl.when(n + 1 < pl.num_programs(1))
    def _():
        for r in seqs:
            kprev_ref[r] = kbuf_ref[r, tq:tq + WINDOW, :]
            vprev_ref[r] = vbuf_ref[r, :, tq:tq + WINDOW]


def _swa_layer(x, positions, gain, w_qkv, q_gain, k_gain, sinks, w_out):
    b, t, d = x.shape
    tq = min(SWA_TQ, t)
    nb = tq // WINDOW
    hd = SWA_HEAD_DIM
    qw = SWA_HEADS * hd
    kw = SWA_KV_HEADS * hd
    inv_freq = ROPE_THETA ** (-jnp.arange(0, ROT_DIM, 2, dtype=F32) / ROT_DIM)
    ang = positions.astype(F32)[:, None, :] * inv_freq[None, :, None]
    cos_f, sin_f = jnp.cos(ang), jnp.sin(ang)
    log2e = float(np.log2(np.e))
    qgain_f = jnp.broadcast_to((q_gain * (hd ** -0.5 * log2e))[:, None], (hd, tq))
    kgain_f = jnp.broadcast_to(k_gain[:, None], (hd, tq))
    rows = SWA_GROUP * WINDOW
    si = np.arange(2 * WINDOW)[:, None]
    qi = np.arange(WINDOW)[None, :] + WINDOW
    band_ok = (qi - si >= 0) & (qi - si < WINDOW)
    band = jnp.asarray(np.tile(np.where(band_ok, 0.0, -np.inf), (1, SWA_GROUP)), F32)
    band0 = jnp.asarray(np.tile(np.where(band_ok & (si >= WINDOW), 0.0, -np.inf), (1, SWA_GROUP)), F32)
    sink_cols = jnp.repeat((sinks.astype(F32) * log2e).reshape(SWA_KV_HEADS, 1, SWA_GROUP), WINDOW, axis=2)

    n_seq = SWA_ROWS
    row_spec = pl.BlockSpec((n_seq, tq, d), lambda i, j: (i, j, 0))
    tab_spec = pl.BlockSpec((n_seq, ROT_DIM // 2, tq), lambda i, j: (i, 0, j))
    return pl.pallas_call(
        _swa_body,
        out_shape=jax.ShapeDtypeStruct((b, t, d), F32),
        grid=(b // n_seq, t // tq),
        in_specs=[
            row_spec,
            _const_spec((1, d)),
            _const_spec((qw + 2 * kw, d)),
            _const_spec((hd, tq)),
            _const_spec((hd, tq)),
            tab_spec, tab_spec,
            _const_spec((2 * WINDOW, rows)),
            _const_spec((2 * WINDOW, rows)),
            _const_spec((SWA_KV_HEADS, 1, rows)),
            _const_spec((d, d)),
        ],
        out_specs=row_spec,
        scratch_shapes=[
            pltpu.VMEM((n_seq, SWA_KV_HEADS, nb, kw, rows), BF16),
            pltpu.VMEM((n_seq, WINDOW + tq, kw), BF16),
            pltpu.VMEM((n_seq, kw, WINDOW + tq), BF16),
            pltpu.VMEM((n_seq, WINDOW, kw), BF16),
            pltpu.VMEM((n_seq, kw, WINDOW), BF16),
            pltpu.VMEM((n_seq, qw, tq), BF16),
        ],
        compiler_params=_params("arbitrary", "arbitrary"),
        name="swa",
    )(x, gain.reshape(1, d), w_qkv.T.astype(BF16), qgain_f, kgain_f, cos_f, sin_f,
      band, band0, sink_cols, w_out.astype(BF16))


def kernel(x, mem, positions, mix_norm, hyb_w_in, gla_w_gate2, gla_gate_bias, gla_out_gain, conv_w,
           hyb_w_out, swa_w_qkv, swa_q_gain, swa_k_gain, swa_sinks, swa_w_out, mem_norm, xa_norm, xa_wq,
           xa_wkv, xa_q_gain, xa_k_gain, xa_wo, ffn_norm, ffn_w_gate_up, ffn_w_down):
    b, t, d = x.shape
    depth = mix_norm.shape[0]
    xa_wq_t = jnp.swapaxes(xa_wq, 1, 2).astype(BF16)
    xa_wkv_b, xa_wo_b = xa_wkv.astype(BF16), xa_wo.astype(BF16)
    ffn_wgu_b, ffn_wd_b = ffn_w_gate_up.astype(BF16), ffn_w_down.astype(BF16)
    for layer in range(depth):
        i = layer // 2
        if layer % 2 == 0:
            x = _gla_layer(x, mix_norm[layer], hyb_w_in[i], gla_w_gate2[i], gla_gate_bias[i],
                           gla_out_gain[i], conv_w[i], hyb_w_out[i])
        else:
            x = _swa_layer(x, positions, mix_norm[layer], swa_w_qkv[i], swa_q_gain[i], swa_k_gain[i],
                           swa_sinks[i], swa_w_out[i])
        x = _cross_attention(x, mem, layer, mem_norm, xa_norm, xa_wq_t, xa_wkv_b, xa_q_gain, xa_k_gain,
                             xa_wo_b)
        x = _ffn(x.reshape(b * t, d), layer, ffn_norm, ffn_wgu_b, ffn_wd_b).reshape(b, t, d)
    return x
```

```python
import numpy as np
import jax
import jax.numpy as jnp
from jax import lax
from jax.experimental import pallas as pl
from jax.experimental.pallas import tpu as pltpu

F32 = jnp.float32
BF16 = jnp.bfloat16

D_MODEL = 1024
RMS_EPS = 1e-6
GLA_HEADS = 4
GLA_DK = 64
GLA_DV = 128
GLA_RANK = 16
GLA_TAU = 16.0
GLA_CHUNK = 64
GLA_KW = GLA_HEADS * GLA_DK
GLA_VW = GLA_HEADS * GLA_DV
CONV_DIM = 512
CONV_WIDTH = 3
RANK_PAD = 128
SWA_HEADS = 16
SWA_KV_HEADS = 4
SWA_GROUP = SWA_HEADS // SWA_KV_HEADS
SWA_HEAD_DIM = 64
WINDOW = 128
ROT_DIM = 16
ROPE_THETA = 500000.0
XA_HEADS = 4
XA_HEAD_DIM = 256
FFN_HIDDEN = 2816
FFN_CHUNK = 256

SUBLANES = 8
VMEM_LIMIT = 56 * 1024 * 1024

GLA_TM = 256
GLA_ROWS = 4
SWA_TQ = 512
SWA_ROWS = 2
SWA_WAVE = 8
SWA_LOOKAHEAD = 1
XA_TM = 1024
XA_KV_ROWS = 2
XA_SUB = 2
FFN_TM = 1024
FFN_SUB = 2


def _rms(x, gain):
    return x * lax.rsqrt(jnp.mean(x * x, axis=-1, keepdims=True) + RMS_EPS) * gain


def _dot(a, b):
    return jnp.dot(a, b, preferred_element_type=F32)


def _dot_nt(a, b):
    return lax.dot_general(a, b, (((1,), (1,)), ((), ())), preferred_element_type=F32)


def _dot_tn(a, b):
    return lax.dot_general(a, b, (((0,), (0,)), ((), ())), preferred_element_type=F32)


def _split_bf16(x):
    hi = x.astype(BF16)
    lo = (x - hi.astype(F32)).astype(BF16)
    return hi, lo


def _const_spec(shape):
    return pl.BlockSpec(shape, lambda *_: (0,) * len(shape))


def _layer_spec(shape, layer):
    return pl.BlockSpec((1,) + tuple(shape), lambda *_: (layer,) + (0,) * len(shape))


def _params(*semantics):
    return pltpu.CompilerParams(dimension_semantics=semantics, vmem_limit_bytes=VMEM_LIMIT)


def _ffn_body(x_ref, gain_ref, wgu_ref, wd_ref, o_ref, h_ref, acc_ref):
    ts = x_ref.shape[0] // FFN_SUB
    for i in range(FFN_SUB):
        x = x_ref[i * ts:(i + 1) * ts, :]
        h_ref[i] = _rms(x, gain_ref[0]).astype(BF16)
        acc_ref[i] = x
    for c in range(FFN_HIDDEN // FFN_CHUNK):
        lo = c * FFN_CHUNK
        for i in range(FFN_SUB):
            h = h_ref[i]
            g = _dot(h, wgu_ref[0, :, lo:lo + FFN_CHUNK])
            u = _dot(h, wgu_ref[0, :, FFN_HIDDEN + lo:FFN_HIDDEN + lo + FFN_CHUNK])
            a = (g / (1.0 + jnp.exp(-g)) * u).astype(BF16)
            acc_ref[i] += _dot(a, wd_ref[0, lo:lo + FFN_CHUNK, :])
    for i in range(FFN_SUB):
        o_ref[i * ts:(i + 1) * ts, :] = acc_ref[i]


def _ffn(x2, layer, ffn_norm, w_gate_up, w_down):
    m, d = x2.shape
    tm = min(FFN_TM, m)
    return pl.pallas_call(
        _ffn_body,
        out_shape=jax.ShapeDtypeStruct((m, d), F32),
        grid=(m // tm,),
        in_specs=[
            pl.BlockSpec((tm, d), lambda i: (i, 0)),
            _layer_spec((1, d), layer),
            _layer_spec((d, 2 * FFN_HIDDEN), layer),
            _layer_spec((FFN_HIDDEN, d), layer),
        ],
        out_specs=pl.BlockSpec((tm, d), lambda i: (i, 0)),
        scratch_shapes=[pltpu.VMEM((FFN_SUB, tm // FFN_SUB, d), BF16),
                        pltpu.VMEM((FFN_SUB, tm // FFN_SUB, d), F32)],
        compiler_params=_params("parallel"),
        name="ffn",
    )(x2, ffn_norm[:, None, :], w_gate_up, w_down)


def _xa_kv_body(mem_ref, gain_ref, wkv_ref, kgain_ref, k_ref, vt_ref):
    rows, mem = mem_ref.shape[0], mem_ref.shape[1]
    mh = jnp.concatenate([_rms(mem_ref[r], gain_ref[0]).astype(BF16) for r in range(rows)], axis=0)
    kv = _dot(mh, wkv_ref[0])
    for r in range(rows):
        kv_r = kv[r * mem:(r + 1) * mem]
        for h in range(XA_HEADS):
            lo = h * XA_HEAD_DIM
            k_ref[r, :, lo:lo + XA_HEAD_DIM] = _rms(kv_r[:, lo:lo + XA_HEAD_DIM], kgain_ref[0]).astype(BF16)
        vt_ref[r] = kv_r[:, D_MODEL:].T.astype(BF16)


def _xa_body(x_ref, gain_ref, wq_t_ref, qgain_ref, k_ref, vt_ref, wo_ref, o_ref, att_ref):
    hd = XA_HEAD_DIM
    mem = k_ref.shape[1]
    ts = x_ref.shape[1] // XA_SUB
    subs = range(XA_SUB)
    xs = [x_ref[0, i * ts:(i + 1) * ts, :] for i in subs]
    q_t = [_dot_nt(wq_t_ref[0], _rms(x, gain_ref[0]).astype(BF16)) for x in xs]
    qscale = hd ** -0.5 * float(np.log2(np.e))
    ones_rows = jnp.ones((2 * SUBLANES, mem), BF16)

    units = [(i, head) for i in subs for head in range(XA_HEADS)]
    scores = []
    for i, head in units:
        z = q_t[i][head * hd:(head + 1) * hd, :]
        ss = jnp.sum(z * z, axis=0, keepdims=True)
        z = z * (lax.rsqrt(ss * (1.0 / hd) + RMS_EPS) * qscale) * qgain_ref[0]
        scores.append(_dot(k_ref[0, :, head * hd:(head + 1) * hd], z.astype(BF16)))

    for i in subs:
        sub_units = [(u, unit) for u, unit in enumerate(units) if unit[0] == i]
        probs = [jnp.exp2(scores[u] - jnp.max(scores[u], axis=0, keepdims=True)).astype(BF16)
                 for u, _ in sub_units]
        for (u, (_, head)), p in zip(sub_units, probs):
            v_aug = jnp.concatenate([vt_ref[0, head * hd:(head + 1) * hd, :], ones_rows], axis=0)
            o_aug = _dot(v_aug, p)
            o = o_aug[0:hd, :] * (1.0 / o_aug[hd:hd + 1, :])
            att_ref[i, head * hd:(head + 1) * hd, :] = o.astype(BF16)
        o_ref[0, i * ts:(i + 1) * ts, :] = xs[i] + _dot_tn(att_ref[i], wo_ref[0])


def _cross_attention(x, mem, layer, mem_norm, xa_norm, wq_t, wkv, q_gain, k_gain, wo):
    b, t, d = x.shape
    m = mem.shape[1]
    k, vt = pl.pallas_call(
        _xa_kv_body,
        out_shape=(jax.ShapeDtypeStruct((b, m, d), BF16), jax.ShapeDtypeStruct((b, d, m), BF16)),
        grid=(b // XA_KV_ROWS,),
        in_specs=[
            pl.BlockSpec((XA_KV_ROWS, m, d), lambda i: (i, 0, 0)),
            _layer_spec((1, d), layer),
            _layer_spec((d, 2 * d), layer),
            _layer_spec((1, XA_HEAD_DIM), layer),
        ],
        out_specs=(pl.BlockSpec((XA_KV_ROWS, m, d), lambda i: (i, 0, 0)),
                   pl.BlockSpec((XA_KV_ROWS, d, m), lambda i: (i, 0, 0))),
        compiler_params=_params("parallel"),
        name="xa_kv",
    )(mem, mem_norm[:, None, :], wkv, k_gain[:, None, :])
    tm = min(XA_TM, t)
    return pl.pallas_call(
        _xa_body,
        out_shape=jax.ShapeDtypeStruct((b, t, d), F32),
        grid=(b, t // tm),
        in_specs=[
            pl.BlockSpec((1, tm, d), lambda i, j: (i, j, 0)),
            _layer_spec((1, d), layer),
            _layer_spec((d, d), layer),
            _layer_spec((XA_HEAD_DIM, 1), layer),
            pl.BlockSpec((1, m, d), lambda i, j: (i, 0, 0)),
            pl.BlockSpec((1, d, m), lambda i, j: (i, 0, 0)),
            _layer_spec((d, d), layer),
        ],
        out_specs=pl.BlockSpec((1, tm, d), lambda i, j: (i, j, 0)),
        scratch_shapes=[pltpu.VMEM((XA_SUB, d, tm // XA_SUB), BF16)],
        compiler_params=_params("parallel", "parallel"),
        name="xa",
    )(x, xa_norm[:, None, :], wq_t, q_gain[:, :, None], k, vt, wo)


_GLA_Q, _GLA_K, _GLA_V, _GLA_GO, _GLA_CB, _GLA_CC, _GLA_CI, _GLA_GL = (
    0, 256, 512, 1024, 1536, 2048, 2560, 3072)
GLA_PROJ_W = _GLA_GL + RANK_PAD


def _gla_body(x_ref, gain_ref, win_ref, wg2_ref, gbias_ref, ogain_ref, convw_ref, wout_ref,
              tri_ref, ones_ref, hmask_ref, smask_ref, o_ref,
              st_ref, ubuf_ref, utail_ref, cat_ref):
    rows, tm = x_ref.shape[0], x_ref.shape[1]
    seqs = range(rows)

    @pl.when(pl.program_id(1) == 0)
    def _():
        st_ref[...] = jnp.zeros_like(st_ref)
        utail_ref[...] = jnp.zeros_like(utail_ref)

    xs, projs, bs, b_lasts = [], [], [], []
    q_in, k_in, k_out, decay, v = [], [], [], [], []

    def project(r):
        xs.append(x_ref[r])
        projs.append(_dot(_rms(xs[r], gain_ref[...]).astype(BF16), win_ref[...]))

    def gates(r):
        logits = _dot(projs[r][:, _GLA_GL:_GLA_GL + RANK_PAD].astype(BF16), wg2_ref[...]) + gbias_ref[...]
        log_a = (jnp.minimum(logits, 0.0) - jnp.log(1.0 + jnp.exp(-jnp.abs(logits)))) * (1.0 / GLA_TAU)
        la_hi, la_lo = _split_bf16(log_a)
        bs.append(_dot(tri_ref[...], la_hi) + _dot(tri_ref[...], la_lo))
        b_lasts.append(_dot(ones_ref[...], la_hi) + _dot(ones_ref[...], la_lo))

    def vector_stage(r):
        proj, b, b_last = projs[r], bs[r], b_lasts[r]
        q = proj[:, _GLA_Q:_GLA_Q + GLA_KW] * (GLA_DK ** -0.5)
        k = proj[:, _GLA_K:_GLA_K + GLA_KW]
        v.append(proj[:, _GLA_V:_GLA_V + GLA_VW].astype(BF16))
        q_in.append((q * jnp.exp(b)).astype(BF16))
        k_in.append((k * jnp.exp(-b)).astype(BF16))
        k_out.append((k * jnp.exp(b_last - b)).astype(BF16))
        decay.append(jnp.exp(b_last))
        u = proj[:, _GLA_CC:_GLA_CC + CONV_DIM] * proj[:, _GLA_CI:_GLA_CI + CONV_DIM]
        ubuf_ref[r, 0:SUBLANES, :] = utail_ref[r]
        ubuf_ref[r, SUBLANES:SUBLANES + tm, :] = u
        y = (convw_ref[0:1, :] * ubuf_ref[r, SUBLANES - 2:SUBLANES - 2 + tm, :]
             + convw_ref[1:2, :] * ubuf_ref[r, SUBLANES - 1:SUBLANES - 1 + tm, :]
             + convw_ref[2:3, :] * u)
        cat_ref[r, :, GLA_VW:GLA_VW + CONV_DIM] = (proj[:, _GLA_CB:_GLA_CB + CONV_DIM] * y).astype(BF16)

    project(0)
    for r in seqs:
        if r + 1 < rows:
            project(r + 1)
        gates(r)
        vector_stage(r)

    o_inter = [[] for _ in seqs]
    for c in range(tm // GLA_CHUNK):
        r0 = c * GLA_CHUNK
        for r in seqs:
            st = st_ref[r]
            o_inter[r].append(_dot_nt(q_in[r][r0:r0 + GLA_CHUNK], st.astype(BF16)))
            kv_t = _dot_tn(v[r][r0:r0 + GLA_CHUNK], k_out[r][r0:r0 + GLA_CHUNK])
            st_ref[r] = st * decay[r][r0:r0 + 1, :] + jnp.where(smask_ref[...] > 0.5, kv_t, 0.0)
    o_inter = [jnp.concatenate(o, axis=0) for o in o_inter]

    causal = tri_ref[...] > 0.5
    for hd in range(GLA_HEADS):
        lo = hd * GLA_DV
        for r in seqs:
            q_h = jnp.where(hmask_ref[hd:hd + 1, :] > 0.5, q_in[r], jnp.zeros_like(q_in[r]))
            scores = jnp.where(causal, _dot_nt(q_h, k_in[r]), 0.0)
            o_h = _dot(scores.astype(BF16), v[r][:, lo:lo + GLA_DV]) + o_inter[r][:, lo:lo + GLA_DV]
            g_h = projs[r][:, _GLA_GO + lo:_GLA_GO + lo + GLA_DV]
            y_h = _rms(o_h, ogain_ref[...]) * (g_h / (1.0 + jnp.exp(-g_h)))
            cat_ref[r, :, lo:lo + GLA_DV] = y_h.astype(BF16)

    for r in seqs:
        o_ref[r] = xs[r] + _dot(cat_ref[r], wout_ref[...])

    @pl.when(pl.program_id(1) + 1 < pl.num_programs(1))
    def _():
        for r in seqs:
            utail_ref[r] = ubuf_ref[r, tm:tm + SUBLANES, :]


def _gla_layer(x, gain, w_in, w_gate2, gate_bias, out_gain, conv_w, w_out):
    b, t, d = x.shape
    tm = min(GLA_TM, t)
    rows = GLA_ROWS
    widths = [GLA_KW, GLA_KW, GLA_VW, GLA_RANK, GLA_VW, CONV_DIM, CONV_DIM, CONV_DIM]
    cuts = [int(c) for c in np.cumsum(widths)[:-1]]
    wq, wk, wv, wgl, wgo, wcb, wcc, wci = jnp.split(w_in, cuts, axis=1)
    wgl = jnp.pad(wgl, ((0, 0), (0, RANK_PAD - GLA_RANK)))
    w_in_r = jnp.concatenate([wq, wk, wv, wgo, wcb, wcc, wci, wgl], axis=1).astype(BF16)
    wg2 = jnp.pad(w_gate2, ((0, RANK_PAD - GLA_RANK), (0, 0))).astype(BF16)

    r = np.arange(tm)
    same_chunk = (r[:, None] // GLA_CHUNK) == (r[None, :] // GLA_CHUNK)
    tri = jnp.asarray(same_chunk & (r[None, :] <= r[:, None]), BF16)
    ones = jnp.asarray(same_chunk, BF16)
    hmask = jnp.asarray(np.arange(GLA_KW)[None, :] // GLA_DK == np.arange(GLA_HEADS)[:, None], F32)
    smask = jnp.asarray(
        (np.arange(GLA_VW)[:, None] // GLA_DV) == (np.arange(GLA_KW)[None, :] // GLA_DK), F32)

    return pl.pallas_call(
        _gla_body,
        out_shape=jax.ShapeDtypeStruct((b, t, d), F32),
        grid=(b // rows, t // tm),
        in_specs=[
            pl.BlockSpec((rows, tm, d), lambda i, j: (i, j, 0)),
            _const_spec((1, d)),
            _const_spec((d, GLA_PROJ_W)),
            _const_spec((RANK_PAD, GLA_KW)),
            _const_spec((1, GLA_KW)),
            _const_spec((1, GLA_DV)),
            _const_spec((CONV_WIDTH, CONV_DIM)),
            _const_spec((GLA_VW + CONV_DIM, d)),
            _const_spec((tm, tm)),
            _const_spec((tm, tm)),
            _const_spec((GLA_HEADS, GLA_KW)),
            _const_spec((GLA_VW, GLA_KW)),
        ],
        out_specs=pl.BlockSpec((rows, tm, d), lambda i, j: (i, j, 0)),
        scratch_shapes=[
            pltpu.VMEM((rows, GLA_VW, GLA_KW), F32),
            pltpu.VMEM((rows, tm + SUBLANES, CONV_DIM), F32),
            pltpu.VMEM((rows, SUBLANES, CONV_DIM), F32),
            pltpu.VMEM((rows, tm, GLA_VW + CONV_DIM), BF16),
        ],
        compiler_params=_params("parallel", "arbitrary"),
        name="gla_conv",
    )(x, gain.reshape(1, d), w_in_r, wg2, gate_bias.reshape(1, GLA_KW), out_gain.reshape(1, GLA_DV),
      conv_w, w_out.astype(BF16), tri, ones, hmask, smask)


def _norm_rope_heads(z, n_heads, gain, cos_f, sin_f):
    hd, half = SWA_HEAD_DIM, ROT_DIM // 2
    out = []
    for head in range(n_heads):
        zh = z[head * hd:(head + 1) * hd, :]
        ss = jnp.sum(zh * zh, axis=0, keepdims=True)
        zh = zh * lax.rsqrt(ss * (1.0 / hd) + RMS_EPS) * gain
        x1, x2 = zh[0:half], zh[half:ROT_DIM]
        out.append(jnp.concatenate(
            [x1 * cos_f - x2 * sin_f, x2 * cos_f + x1 * sin_f, zh[ROT_DIM:]], axis=0))
    return out


def _swa_body(x_ref, gain_ref, wqkv_t_ref, qgain_ref, kgain_ref, cos_ref, sin_ref,
              band_ref, band0_ref, sink_ref, wout_ref, o_ref,
              qpad_ref, kbuf_ref, vbuf_ref, kprev_ref, vprev_ref, att_ref):
    n_seq, tq = x_ref.shape[0], x_ref.shape[1]
    seqs = range(n_seq)
    nb = tq // WINDOW
    n = pl.program_id(1)
    hd = SWA_HEAD_DIM
    qw = SWA_HEADS * hd
    kw = SWA_KV_HEADS * hd

    @pl.when((pl.program_id(0) == 0) & (n == 0))
    def _():
        qpad_ref[...] = jnp.zeros_like(qpad_ref)

    @pl.when(n == 0)
    def _():
        kprev_ref[...] = jnp.zeros_like(kprev_ref)
        vprev_ref[...] = jnp.zeros_like(vprev_ref)

    xs = [x_ref[r] for r in seqs]
    qkv_t = [_dot_nt(wqkv_t_ref[...], _rms(x, gain_ref[...]).astype(BF16)) for x in xs]

    for r in seqs:
        cos_f, sin_f = cos_ref[r], sin_ref[r]
        k_heads = _norm_rope_heads(qkv_t[r][qw:qw + kw], SWA_KV_HEADS, kgain_ref[...], cos_f, sin_f)
        kbuf_ref[r, 0:WINDOW, :] = kprev_ref[r]
        kbuf_ref[r, WINDOW:WINDOW + tq, :] = jnp.concatenate(k_heads, axis=0).T.astype(BF16)
        vbuf_ref[r, :, 0:WINDOW] = vprev_ref[r]
        vbuf_ref[r, :, WINDOW:WINDOW + tq] = qkv_t[r][qw + kw:, :].astype(BF16)
        q_heads = _norm_rope_heads(qkv_t[r][:qw], SWA_HEADS, qgain_ref[...], cos_f, sin_f)
        for head, z in enumerate(q_heads):
            g, j = divmod(head, SWA_GROUP)
            z = z.astype(BF16)
            for i in range(nb):
                qpad_ref[r, g, i, g * hd:(g + 1) * hd, j * WINDOW:(j + 1) * WINDOW] = (
                    z[:, i * WINDOW:(i + 1) * WINDOW])

    bias0 = jnp.where(n > 0, band_ref[...], band0_ref[...])

    units = [(r, g, i) for g in range(SWA_KV_HEADS) for i in range(nb) for r in seqs]

    def scores(unit):
        r, g, i = unit
        return _dot(kbuf_ref[r, i * WINDOW:(i + 2) * WINDOW, :], qpad_ref[r, g, i])

    ones_rows = jnp.ones((2 * SUBLANES, 2 * WINDOW), BF16)
    waves = [units[w:w + SWA_WAVE] for w in range(0, len(units), SWA_WAVE)]
    pending = [[scores(unit) for unit in wave] for wave in waves[:SWA_LOOKAHEAD]]
    for w, wave in enumerate(waves):
        s_wave = pending.pop(0)
        if w + SWA_LOOKAHEAD < len(waves):
            pending.append([scores(unit) for unit in waves[w + SWA_LOOKAHEAD]])
        stats, probs = [], []
        for (r, g, i), s in zip(wave, s_wave):
            s = s + (bias0 if i == 0 else band_ref[...])
            sink = sink_ref[g]
            m = jnp.maximum(jnp.max(s, axis=0, keepdims=True), sink)
            if w + SWA_LOOKAHEAD < len(waves):
                m = m + jnp.minimum(jnp.abs(pending[-1][0][0:1, :]), 0.0)
            probs.append(jnp.exp2(s - m).astype(BF16))
            stats.append((sink, m))
        for (r, g, i), p, (sink, m) in zip(wave, probs, stats):
            r0 = i * WINDOW
            v_aug = jnp.concatenate(
                [vbuf_ref[r, g * hd:(g + 1) * hd, r0:r0 + 2 * WINDOW], ones_rows], axis=0)
            o_aug = _dot(v_aug, p)
            denom = o_aug[hd:hd + 1, :] + jnp.exp2(sink - m)
            o = o_aug[0:hd, :] * (1.0 / denom)
            for j in range(SWA_GROUP):
                head = g * SWA_GROUP + j
                att_ref[r, head * hd:(head + 1) * hd, r0:r0 + WINDOW] = (
                    o[:, j * WINDOW:(j + 1) * WINDOW].astype(BF16))

    for r in seqs:
        o_ref[r] = xs[r] + _dot_tn(att_ref[r], wout_ref[...])

    @pl.when(n + 1 < pl.num_programs(1))
    def _():
        for r in seqs:
            kprev_ref[r] = kbuf_ref[r, tq:tq + WINDOW, :]
            vprev_ref[r] = vbuf_ref[r, :, tq:tq + WINDOW]


def _swa_layer(x, positions, gain, w_qkv, q_gain, k_gain, sinks, w_out):
    b, t, d = x.shape
    tq = min(SWA_TQ, t)
    nb = tq // WINDOW
    hd = SWA_HEAD_DIM
    qw = SWA_HEADS * hd
    kw = SWA_KV_HEADS * hd
    inv_freq = ROPE_THETA ** (-jnp.arange(0, ROT_DIM, 2, dtype=F32) / ROT_DIM)
    ang = positions.astype(F32)[:, None, :] * inv_freq[None, :, None]
    cos_f, sin_f = jnp.cos(ang), jnp.sin(ang)
    log2e = float(np.log2(np.e))
    qgain_f = jnp.broadcast_to((q_gain * (hd ** -0.5 * log2e))[:, None], (hd, tq))
    kgain_f = jnp.broadcast_to(k_gain[:, None], (hd, tq))
    rows = SWA_GROUP * WINDOW
    si = np.arange(2 * WINDOW)[:, None]
    qi = np.arange(WINDOW)[None, :] + WINDOW
    band_ok = (qi - si >= 0) & (qi - si < WINDOW)
    band = jnp.asarray(np.tile(np.where(band_ok, 0.0, -np.inf), (1, SWA_GROUP)), F32)
    band0 = jnp.asarray(np.tile(np.where(band_ok & (si >= WINDOW), 0.0, -np.inf), (1, SWA_GROUP)), F32)
    sink_cols = jnp.repeat((sinks.astype(F32) * log2e).reshape(SWA_KV_HEADS, 1, SWA_GROUP), WINDOW, axis=2)

    n_seq = SWA_ROWS
    row_spec = pl.BlockSpec((n_seq, tq, d), lambda i, j: (i, j, 0))
    tab_spec = pl.BlockSpec((n_seq, ROT_DIM // 2, tq), lambda i, j: (i, 0, j))
    return pl.pallas_call(
        _swa_body,
        out_shape=jax.ShapeDtypeStruct((b, t, d), F32),
        grid=(b // n_seq, t // tq),
        in_specs=[
            row_spec,
            _const_spec((1, d)),
            _const_spec((qw + 2 * kw, d)),
            _const_spec((hd, tq)),
            _const_spec((hd, tq)),
            tab_spec, tab_spec,
            _const_spec((2 * WINDOW, rows)),
            _const_spec((2 * WINDOW, rows)),
            _const_spec((SWA_KV_HEADS, 1, rows)),
            _const_spec((d, d)),
        ],
        out_specs=row_spec,
        scratch_shapes=[
            pltpu.VMEM((n_seq, SWA_KV_HEADS, nb, kw, rows), BF16),
            pltpu.VMEM((n_seq, WINDOW + tq, kw), BF16),
            pltpu.VMEM((n_seq, kw, WINDOW + tq), BF16),
            pltpu.VMEM((n_seq, WINDOW, kw), BF16),
            pltpu.VMEM((n_seq, kw, WINDOW), BF16),
            pltpu.VMEM((n_seq, qw, tq), BF16),
        ],
        compiler_params=_params("arbitrary", "arbitrary"),
        name="swa",
    )(x, gain.reshape(1, d), w_qkv.T.astype(BF16), qgain_f, kgain_f, cos_f, sin_f,
      band, band0, sink_cols, w_out.astype(BF16))


def kernel(x, mem, positions, mix_norm, hyb_w_in, gla_w_gate2, gla_gate_bias, gla_out_gain, conv_w,
           hyb_w_out, swa_w_qkv, swa_q_gain, swa_k_gain, swa_sinks, swa_w_out, mem_norm, xa_norm, xa_wq,
           xa_wkv, xa_q_gain, xa_k_gain, xa_wo, ffn_norm, ffn_w_gate_up, ffn_w_down):
    b, t, d = x.shape
    depth = mix_norm.shape[0]
    xa_wq_t = jnp.swapaxes(xa_wq, 1, 2).astype(BF16)
    xa_wkv_b, xa_wo_b = xa_wkv.astype(BF16), xa_wo.astype(BF16)
    ffn_wgu_b, ffn_wd_b = ffn_w_gate_up.astype(BF16), ffn_w_down.astype(BF16)
    for layer in range(depth):
        i = layer // 2
        if layer % 2 == 0:
            x = _gla_layer(x, mix_norm[layer], hyb_w_in[i], gla_w_gate2[i], gla_gate_bias[i],
                           gla_out_gain[i], conv_w[i], hyb_w_out[i])
        else:
            x = _swa_layer(x, positions, mix_norm[layer], swa_w_qkv[i], swa_q_gain[i], swa_k_gain[i],
                           swa_sinks[i], swa_w_out[i])
        x = _cross_attention(x, mem, layer, mem_norm, xa_norm, xa_wq_t, xa_wkv_b, xa_q_gain, xa_k_gain,
                             xa_wo_b)
        x = _ffn(x.reshape(b * t, d), layer, ffn_norm, ffn_wgu_b, ffn_wd_b).reshape(b, t, d)
    return x
```

```python
import numpy as np
import jax
import jax.numpy as jnp
from jax import lax
from jax.experimental import pallas as pl
from jax.experimental.pallas import tpu as pltpu

F32 = jnp.float32
BF16 = jnp.bfloat16

D_MODEL = 1024
RMS_EPS = 1e-6
GLA_HEADS = 4
GLA_DK = 64
GLA_DV = 128
GLA_RANK = 16
GLA_TAU = 16.0
GLA_CHUNK = 64
GLA_KW = GLA_HEADS * GLA_DK
GLA_VW = GLA_HEADS * GLA_DV
CONV_DIM = 512
CONV_WIDTH = 3
RANK_PAD = 128
SWA_HEADS = 16
SWA_KV_HEADS = 4
SWA_GROUP = SWA_HEADS // SWA_KV_HEADS
SWA_HEAD_DIM = 64
WINDOW = 128
ROT_DIM = 16
ROPE_THETA = 500000.0
XA_HEADS = 4
XA_HEAD_DIM = 256
FFN_HIDDEN = 2816
FFN_CHUNK = 256

SUBLANES = 8
VMEM_LIMIT = 56 * 1024 * 1024

GLA_TM = 256
GLA_ROWS = 4
SWA_TQ = 512
SWA_ROWS = 2
SWA_WAVE = 8
SWA_LOOKAHEAD = 1
XA_TM = 1024
XA_KV_ROWS = 2
XA_SUB = 2
FFN_TM = 1024
FFN_SUB = 2


def _rms(x, gain):
    return x * lax.rsqrt(jnp.mean(x * x, axis=-1, keepdims=True) + RMS_EPS) * gain


def _dot(a, b):
    return jnp.dot(a, b, preferred_element_type=F32)


def _dot_nt(a, b):
    return lax.dot_general(a, b, (((1,), (1,)), ((), ())), preferred_element_type=F32)


def _dot_tn(a, b):
    return lax.dot_general(a, b, (((0,), (0,)), ((), ())), preferred_element_type=F32)


def _split_bf16(x):
    hi = x.astype(BF16)
    lo = (x - hi.astype(F32)).astype(BF16)
    return hi, lo


def _const_spec(shape):
    return pl.BlockSpec(shape, lambda *_: (0,) * len(shape))


def _layer_spec(shape, layer):
    return pl.BlockSpec((1,) + tuple(shape), lambda *_: (layer,) + (0,) * len(shape))


def _params(*semantics):
    return pltpu.CompilerParams(dimension_semantics=semantics, vmem_limit_bytes=VMEM_LIMIT)


def _ffn_body(x_ref, gain_ref, wgu_ref, wd_ref, o_ref, h_ref, acc_ref):
    ts = x_ref.shape[0] // FFN_SUB
    for i in range(FFN_SUB):
        x = x_ref[i * ts:(i + 1) * ts, :]
        h_ref[i] = _rms(x, gain_ref[0]).astype(BF16)
        acc_ref[i] = x
    for c in range(FFN_HIDDEN // FFN_CHUNK):
        lo = c * FFN_CHUNK
        for i in range(FFN_SUB):
            h = h_ref[i]
            g = _dot(h, wgu_ref[0, :, lo:lo + FFN_CHUNK])
            u = _dot(h, wgu_ref[0, :, FFN_HIDDEN + lo:FFN_HIDDEN + lo + FFN_CHUNK])
            a = (g / (1.0 + jnp.exp(-g)) * u).astype(BF16)
            acc_ref[i] += _dot(a, wd_ref[0, lo:lo + FFN_CHUNK, :])
    for i in range(FFN_SUB):
        o_ref[i * ts:(i + 1) * ts, :] = acc_ref[i]


def _ffn(x2, layer, ffn_norm, w_gate_up, w_down):
    m, d = x2.shape
    tm = min(FFN_TM, m)
    return pl.pallas_call(
        _ffn_body,
        out_shape=jax.ShapeDtypeStruct((m, d), F32),
        grid=(m // tm,),
        in_specs=[
            pl.BlockSpec((tm, d), lambda i: (i, 0)),
            _layer_spec((1, d), layer),
            _layer_spec((d, 2 * FFN_HIDDEN), layer),
            _layer_spec((FFN_HIDDEN, d), layer),
        ],
        out_specs=pl.BlockSpec((tm, d), lambda i: (i, 0)),
        scratch_shapes=[pltpu.VMEM((FFN_SUB, tm // FFN_SUB, d), BF16),
                        pltpu.VMEM((FFN_SUB, tm // FFN_SUB, d), F32)],
        compiler_params=_params("parallel"),
        name="ffn",
    )(x2, ffn_norm[:, None, :], w_gate_up, w_down)


def _xa_kv_body(mem_ref, gain_ref, wkv_ref, kgain_ref, k_ref, vt_ref):
    rows, mem = mem_ref.shape[0], mem_ref.shape[1]
    mh = jnp.concatenate([_rms(mem_ref[r], gain_ref[0]).astype(BF16) for r in range(rows)], axis=0)
    kv = _dot(mh, wkv_ref[0])
    for r in range(rows):
        kv_r = kv[r * mem:(r + 1) * mem]
        for h in range(XA_HEADS):
            lo = h * XA_HEAD_DIM
            k_ref[r, :, lo:lo + XA_HEAD_DIM] = _rms(kv_r[:, lo:lo + XA_HEAD_DIM], kgain_ref[0]).astype(BF16)
        vt_ref[r] = kv_r[:, D_MODEL:].T.astype(BF16)


def _xa_body(x_ref, gain_ref, wq_t_ref, qgain_ref, k_ref, vt_ref, wo_ref, o_ref, att_ref):
    hd = XA_HEAD_DIM
    mem = k_ref.shape[1]
    ts = x_ref.shape[1] // XA_SUB
    subs = range(XA_SUB)
    xs = [x_ref[0, i * ts:(i + 1) * ts, :] for i in subs]
    q_t = [_dot_nt(wq_t_ref[0], _rms(x, gain_ref[0]).astype(BF16)) for x in xs]
    qscale = hd ** -0.5 * float(np.log2(np.e))
    ones_rows = jnp.ones((2 * SUBLANES, mem), BF16)

    units = [(i, head) for i in subs for head in range(XA_HEADS)]
    scores = []
    for i, head in units:
        z = q_t[i][head * hd:(head + 1) * hd, :]
        ss = jnp.sum(z * z, axis=0, keepdims=True)
        z = z * (lax.rsqrt(ss * (1.0 / hd) + RMS_EPS) * qscale) * qgain_ref[0]
        scores.append(_dot(k_ref[0, :, head * hd:(head + 1) * hd], z.astype(BF16)))

    for i in subs:
        sub_units = [(u, unit) for u, unit in enumerate(units) if unit[0] == i]
        probs = [jnp.exp2(scores[u] - jnp.max(scores[u], axis=0, keepdims=True)).astype(BF16)
                 for u, _ in sub_units]
        for (u, (_, head)), p in zip(sub_units, probs):
            v_aug = jnp.concatenate([vt_ref[0, head * hd:(head + 1) * hd, :], ones_rows], axis=0)
            o_aug = _dot(v_aug, p)
            o = o_aug[0:hd, :] * (1.0 / o_aug[hd:hd + 1, :])
            att_ref[i, head * hd:(head + 1) * hd, :] = o.astype(BF16)
        o_ref[0, i * ts:(i + 1) * ts, :] = xs[i] + _dot_tn(att_ref[i], wo_ref[0])


def _cross_attention(x, mem, layer, mem_norm, xa_norm, wq_t, wkv, q_gain, k_gain, wo):
    b, t, d = x.shape
    m = mem.shape[1]
    k, vt = pl.pallas_call(
        _xa_kv_body,
        out_shape=(jax.ShapeDtypeStruct((b, m, d), BF16), jax.ShapeDtypeStruct((b, d, m), BF16)),
        grid=(b // XA_KV_ROWS,),
        in_specs=[
            pl.BlockSpec((XA_KV_ROWS, m, d), lambda i: (i, 0, 0)),
            _layer_spec((1, d), layer),
            _layer_spec((d, 2 * d), layer),
            _layer_spec((1, XA_HEAD_DIM), layer),
        ],
        out_specs=(pl.BlockSpec((XA_KV_ROWS, m, d), lambda i: (i, 0, 0)),
                   pl.BlockSpec((XA_KV_ROWS, d, m), lambda i: (i, 0, 0))),
        compiler_params=_params("parallel"),
        name="xa_kv",
    )(mem, mem_norm[:, None, :], wkv, k_gain[:, None, :])
    tm = min(XA_TM, t)
    return pl.pallas_call(
        _xa_body,
        out_shape=jax.ShapeDtypeStruct((b, t, d), F32),
        grid=(b, t // tm),
        in_specs=[
            pl.BlockSpec((1, tm, d), lambda i, j: (i, j, 0)),
            _layer_spec((1, d), layer),
            _layer_spec((d, d), layer),
            _layer_spec((XA_HEAD_DIM, 1), layer),
            pl.BlockSpec((1, m, d), lambda i, j: (i, 0, 0)),
            pl.BlockSpec((1, d, m), lambda i, j: (i, 0, 0)),
            _layer_spec((d, d), layer),
        ],
        out_specs=pl.BlockSpec((1, tm, d), lambda i, j: (i, j, 0)),
        scratch_shapes=[pltpu.VMEM((XA_SUB, d, tm // XA_SUB), BF16)],
        compiler_params=_params("parallel", "parallel"),
        name="xa",
    )(x, xa_norm[:, None, :], wq_t, q_gain[:, :, None], k, vt, wo)


_GLA_Q, _GLA_K, _GLA_V, _GLA_GO, _GLA_CB, _GLA_CC, _GLA_CI, _GLA_GL = (
    0, 256, 512, 1024, 1536, 2048, 2560, 3072)
GLA_PROJ_W = _GLA_GL + RANK_PAD


def _gla_body(x_ref, gain_ref, win_ref, wg2_ref, gbias_ref, ogain_ref, convw_ref, wout_ref,
              tri_ref, hmask_ref, smask_ref, o_ref,
              st_ref, ubuf_ref, utail_ref, cat_ref):
    rows, tm = x_ref.shape[0], x_ref.shape[1]
    seqs = range(rows)

    @pl.when(pl.program_id(1) == 0)
    def _():
        st_ref[...] = jnp.zeros_like(st_ref)
        utail_ref[...] = jnp.zeros_like(utail_ref)

    xs, projs, bs, b_lasts = [], [], [], []
    q_in, k_in, k_out, decay, v = [], [], [], [], []

    def project(r):
        xs.append(x_ref[r])
        projs.append(_dot(_rms(xs[r], gain_ref[...]).astype(BF16), win_ref[...]))

    def gates(r):
        logits = _dot(projs[r][:, _GLA_GL:_GLA_GL + RANK_PAD].astype(BF16), wg2_ref[...]) + gbias_ref[...]
        log_a = (jnp.minimum(logits, 0.0) - jnp.log(1.0 + jnp.exp(-jnp.abs(logits)))) * (1.0 / GLA_TAU)
        cum = _dot(tri_ref[...], jnp.concatenate(_split_bf16(log_a), axis=1))
        b = cum[:, :GLA_KW] + cum[:, GLA_KW:]
        bs.append(b)
        b_lasts.append(jnp.concatenate(
            [jnp.broadcast_to(b[c + GLA_CHUNK - 1:c + GLA_CHUNK, :], (GLA_CHUNK, GLA_KW))
             for c in range(0, tm, GLA_CHUNK)], axis=0))

    def vector_stage(r):
        proj, b, b_last = projs[r], bs[r], b_lasts[r]
        q = proj[:, _GLA_Q:_GLA_Q + GLA_KW] * (GLA_DK ** -0.5)
        k = proj[:, _GLA_K:_GLA_K + GLA_KW]
        v.append(proj[:, _GLA_V:_GLA_V + GLA_VW].astype(BF16))
        q_in.append((q * jnp.exp(b)).astype(BF16))
        k_in.append((k * jnp.exp(-b)).astype(BF16))
        k_out.append((k * jnp.exp(b_last - b)).astype(BF16))
        decay.append(jnp.exp(b_last))
        u = proj[:, _GLA_CC:_GLA_CC + CONV_DIM] * proj[:, _GLA_CI:_GLA_CI + CONV_DIM]
        ubuf_ref[r, 0:SUBLANES, :] = utail_ref[r]
        ubuf_ref[r, SUBLANES:SUBLANES + tm, :] = u
        y = (convw_ref[0:1, :] * ubuf_ref[r, SUBLANES - 2:SUBLANES - 2 + tm, :]
             + convw_ref[1:2, :] * ubuf_ref[r, SUBLANES - 1:SUBLANES - 1 + tm, :]
             + convw_ref[2:3, :] * u)
        cat_ref[r, :, GLA_VW:GLA_VW + CONV_DIM] = (proj[:, _GLA_CB:_GLA_CB + CONV_DIM] * y).astype(BF16)

    project(0)
    for r in seqs:
        if r + 1 < rows:
            project(r + 1)
        gates(r)
        vector_stage(r)

    o_inter = [[] for _ in seqs]
    for c in range(tm // GLA_CHUNK):
        r0 = c * GLA_CHUNK
        for r in seqs:
            st = st_ref[r]
            o_inter[r].append(_dot_nt(q_in[r][r0:r0 + GLA_CHUNK], st.astype(BF16)))
            v_c, k_c = v[r][r0:r0 + GLA_CHUNK], k_out[r][r0:r0 + GLA_CHUNK]
            half = GLA_VW // 2
            kv_t = jnp.concatenate([_dot_tn(v_c[:, :half], k_c), _dot_tn(v_c[:, half:], k_c)], axis=0)
            st_ref[r] = st * decay[r][r0:r0 + 1, :] + jnp.where(smask_ref[...] > 0.5, kv_t, 0.0)
    o_inter = [jnp.concatenate(o, axis=0) for o in o_inter]

    causal = tri_ref[...] > 0.5
    for hd in range(GLA_HEADS):
        lo = hd * GLA_DV
        for r in seqs:
            q_h = jnp.where(hmask_ref[hd:hd + 1, :] > 0.5, q_in[r], jnp.zeros_like(q_in[r]))
            scores = jnp.where(causal, _dot_nt(q_h, k_in[r]), 0.0)
            o_h = _dot(scores.astype(BF16), v[r][:, lo:lo + GLA_DV]) + o_inter[r][:, lo:lo + GLA_DV]
            g_h = projs[r][:, _GLA_GO + lo:_GLA_GO + lo + GLA_DV]
            y_h = _rms(o_h, ogain_ref[...]) * (g_h / (1.0 + jnp.exp(-g_h)))
            cat_ref[r, :, lo:lo + GLA_DV] = y_h.astype(BF16)

    for r in seqs:
        o_ref[r] = xs[r] + _dot(cat_ref[r], wout_ref[...])

    @pl.when(pl.program_id(1) + 1 < pl.num_programs(1))
    def _():
        for r in seqs:
            utail_ref[r] = ubuf_ref[r, tm:tm + SUBLANES, :]


def _gla_layer(x, gain, w_in, w_gate2, gate_bias, out_gain, conv_w, w_out):
    b, t, d = x.shape
    tm = min(GLA_TM, t)
    rows = GLA_ROWS
    widths = [GLA_KW, GLA_KW, GLA_VW, GLA_RANK, GLA_VW, CONV_DIM, CONV_DIM, CONV_DIM]
    cuts = [int(c) for c in np.cumsum(widths)[:-1]]
    wq, wk, wv, wgl, wgo, wcb, wcc, wci = jnp.split(w_in, cuts, axis=1)
    wgl = jnp.pad(wgl, ((0, 0), (0, RANK_PAD - GLA_RANK)))
    w_in_r = jnp.concatenate([wq, wk, wv, wgo, wcb, wcc, wci, wgl], axis=1).astype(BF16)
    wg2 = jnp.pad(w_gate2, ((0, RANK_PAD - GLA_RANK), (0, 0))).astype(BF16)

    r = np.arange(tm)
    same_chunk = (r[:, None] // GLA_CHUNK) == (r[None, :] // GLA_CHUNK)
    tri = jnp.asarray(same_chunk & (r[None, :] <= r[:, None]), BF16)
    hmask = jnp.asarray(np.arange(GLA_KW)[None, :] // GLA_DK == np.arange(GLA_HEADS)[:, None], F32)
    smask = jnp.asarray(
        (np.arange(GLA_VW)[:, None] // GLA_DV) == (np.arange(GLA_KW)[None, :] // GLA_DK), F32)

    return pl.pallas_call(
        _gla_body,
        out_shape=jax.ShapeDtypeStruct((b, t, d), F32),
        grid=(b // rows, t // tm),
        in_specs=[
            pl.BlockSpec((rows, tm, d), lambda i, j: (i, j, 0)),
            _const_spec((1, d)),
            _const_spec((d, GLA_PROJ_W)),
            _const_spec((RANK_PAD, GLA_KW)),
            _const_spec((1, GLA_KW)),
            _const_spec((1, GLA_DV)),
            _const_spec((CONV_WIDTH, CONV_DIM)),
            _const_spec((GLA_VW + CONV_DIM, d)),
            _const_spec((tm, tm)),
            _const_spec((GLA_HEADS, GLA_KW)),
            _const_spec((GLA_VW, GLA_KW)),
        ],
        out_specs=pl.BlockSpec((rows, tm, d), lambda i, j: (i, j, 0)),
        scratch_shapes=[
            pltpu.VMEM((rows, GLA_VW, GLA_KW), F32),
            pltpu.VMEM((rows, tm + SUBLANES, CONV_DIM), F32),
            pltpu.VMEM((rows, SUBLANES, CONV_DIM), F32),
            pltpu.VMEM((rows, tm, GLA_VW + CONV_DIM), BF16),
        ],
        compiler_params=_params("parallel", "arbitrary"),
        name="gla_conv",
    )(x, gain.reshape(1, d), w_in_r, wg2, gate_bias.reshape(1, GLA_KW), out_gain.reshape(1, GLA_DV),
      conv_w, w_out.astype(BF16), tri, hmask, smask)


def _norm_rope_heads(z, n_heads, gain, cos_f, sin_f):
    hd, half = SWA_HEAD_DIM, ROT_DIM // 2
    out = []
    for head in range(n_heads):
        zh = z[head * hd:(head + 1) * hd, :]
        ss = jnp.sum(zh * zh, axis=0, keepdims=True)
        zh = zh * lax.rsqrt(ss * (1.0 / hd) + RMS_EPS) * gain
        x1, x2 = zh[0:half], zh[half:ROT_DIM]
        out.append(jnp.concatenate(
            [x1 * cos_f - x2 * sin_f, x2 * cos_f + x1 * sin_f, zh[ROT_DIM:]], axis=0))
    return out


def _swa_body(x_ref, gain_ref, wqkv_t_ref, qgain_ref, kgain_ref, cos_ref, sin_ref,
              band_ref, band0_ref, sink_ref, wout_ref, o_ref,
              qpad_ref, kbuf_ref, vbuf_ref, kprev_ref, vprev_ref, att_ref):
    n_seq, tq = x_ref.shape[0], x_ref.shape[1]
    seqs = range(n_seq)
    nb = tq // WINDOW
    n = pl.program_id(1)
    hd = SWA_HEAD_DIM
    qw = SWA_HEADS * hd
    kw = SWA_KV_HEADS * hd

    @pl.when((pl.program_id(0) == 0) & (n == 0))
    def _():
        qpad_ref[...] = jnp.zeros_like(qpad_ref)

    @pl.when(n == 0)
    def _():
        kprev_ref[...] = jnp.zeros_like(kprev_ref)
        vprev_ref[...] = jnp.zeros_like(vprev_ref)

    xs = [x_ref[r] for r in seqs]
    qkv_t = [_dot_nt(wqkv_t_ref[...], _rms(x, gain_ref[...]).astype(BF16)) for x in xs]

    for r in seqs:
        cos_f, sin_f = cos_ref[r], sin_ref[r]
        k_heads = _norm_rope_heads(qkv_t[r][qw:qw + kw], SWA_KV_HEADS, kgain_ref[...], cos_f, sin_f)
        kbuf_ref[r, 0:WINDOW, :] = kprev_ref[r]
        kbuf_ref[r, WINDOW:WINDOW + tq, :] = jnp.concatenate(k_heads, axis=0).T.astype(BF16)
        vbuf_ref[r, :, 0:WINDOW] = vprev_ref[r]
        vbuf_ref[r, :, WINDOW:WINDOW + tq] = qkv_t[r][qw + kw:, :].astype(BF16)
        q_heads = _norm_rope_heads(qkv_t[r][:qw], SWA_HEADS, qgain_ref[...], cos_f, sin_f)
        for head, z in enumerate(q_heads):
            g, j = divmod(head, SWA_GROUP)
            z = z.astype(BF16)
            for i in range(nb):
                qpad_ref[r, g, i, g * hd:(g + 1) * hd, j * WINDOW:(j + 1) * WINDOW] = (
                    z[:, i * WINDOW:(i + 1) * WINDOW])

    bias0 = jnp.where(n > 0, band_ref[...], band0_ref[...])

    units = [(r, g, i) for g in range(SWA_KV_HEADS) for i in range(nb) for r in seqs]

    def scores(unit):
        r, g, i = unit
        return _dot(kbuf_ref[r, i * WINDOW:(i + 2) * WINDOW, :], qpad_ref[r, g, i])

    ones_rows = jnp.ones((2 * SUBLANES, 2 * WINDOW), BF16)
    waves = [units[w:w + SWA_WAVE] for w in range(0, len(units), SWA_WAVE)]
    pending = [[scores(unit) for unit in wave] for wave in waves[:SWA_LOOKAHEAD]]
    for w, wave in enumerate(waves):
        s_wave = pending.pop(0)
        if w + SWA_LOOKAHEAD < len(waves):
            pending.append([scores(unit) for unit in waves[w + SWA_LOOKAHEAD]])
        stats, probs = [], []
        for (r, g, i), s in zip(wave, s_wave):
            s = s + (bias0 if i == 0 else band_ref[...])
            sink = sink_ref[g]
            m = jnp.maximum(jnp.max(s, axis=0, keepdims=True), sink)
            if w + SWA_LOOKAHEAD < len(waves):
                m = m + jnp.minimum(jnp.abs(pending[-1][0][0:1, :]), 0.0)
            probs.append(jnp.exp2(s - m).astype(BF16))
            stats.append((sink, m))
        for (r, g, i), p, (sink, m) in zip(wave, probs, stats):
            r0 = i * WINDOW
            v_aug = jnp.concatenate(
                [vbuf_ref[r, g * hd:(g + 1) * hd, r0:r0 + 2 * WINDOW], ones_rows], axis=0)
            o_aug = _dot(v_aug, p)
            denom = o_aug[hd:hd + 1, :] + jnp.exp2(sink - m)
            o = o_aug[0:hd, :] * (1.0 / denom)
            for j in range(SWA_GROUP):
                head = g * SWA_GROUP + j
                att_ref[r, head * hd:(head + 1) * hd, r0:r0 + WINDOW] = (
                    o[:, j * WINDOW:(j + 1) * WINDOW].astype(BF16))

    for r in seqs:
        o_ref[r] = xs[r] + _dot_tn(att_ref[r], wout_ref[...])

    @pl.when(n + 1 < pl.num_programs(1))
    def _():
        for r in seqs:
            kprev_ref[r] = kbuf_ref[r, tq:tq + WINDOW, :]
            vprev_ref[r] = vbuf_ref[r, :, tq:tq + WINDOW]


def _swa_layer(x, positions, gain, w_qkv, q_gain, k_gain, sinks, w_out):
    b, t, d = x.shape
    tq = min(SWA_TQ, t)
    nb = tq // WINDOW
    hd = SWA_HEAD_DIM
    qw = SWA_HEADS * hd
    kw = SWA_KV_HEADS * hd
    inv_freq = ROPE_THETA ** (-jnp.arange(0, ROT_DIM, 2, dtype=F32) / ROT_DIM)
    ang = positions.astype(F32)[:, None, :] * inv_freq[None, :, None]
    cos_f, sin_f = jnp.cos(ang), jnp.sin(ang)
    log2e = float(np.log2(np.e))
    qgain_f = jnp.broadcast_to((q_gain * (hd ** -0.5 * log2e))[:, None], (hd, tq))
    kgain_f = jnp.broadcast_to(k_gain[:, None], (hd, tq))
    rows = SWA_GROUP * WINDOW
    si = np.arange(2 * WINDOW)[:, None]
    qi = np.arange(WINDOW)[None, :] + WINDOW
    band_ok = (qi - si >= 0) & (qi - si < WINDOW)
    band = jnp.asarray(np.tile(np.where(band_ok, 0.0, -np.inf), (1, SWA_GROUP)), F32)
    band0 = jnp.asarray(np.tile(np.where(band_ok & (si >= WINDOW), 0.0, -np.inf), (1, SWA_GROUP)), F32)
    sink_cols = jnp.repeat((sinks.astype(F32) * log2e).reshape(SWA_KV_HEADS, 1, SWA_GROUP), WINDOW, axis=2)

    n_seq = SWA_ROWS
    row_spec = pl.BlockSpec((n_seq, tq, d), lambda i, j: (i, j, 0))
    tab_spec = pl.BlockSpec((n_seq, ROT_DIM // 2, tq), lambda i, j: (i, 0, j))
    return pl.pallas_call(
        _swa_body,
        out_shape=jax.ShapeDtypeStruct((b, t, d), F32),
        grid=(b // n_seq, t // tq),
        in_specs=[
            row_spec,
            _const_spec((1, d)),
            _const_spec((qw + 2 * kw, d)),
            _const_spec((hd, tq)),
            _const_spec((hd, tq)),
            tab_spec, tab_spec,
            _const_spec((2 * WINDOW, rows)),
            _const_spec((2 * WINDOW, rows)),
            _const_spec((SWA_KV_HEADS, 1, rows)),
            _const_spec((d, d)),
        ],
        out_specs=row_spec,
        scratch_shapes=[
            pltpu.VMEM((n_seq, SWA_KV_HEADS, nb, kw, rows), BF16),
            pltpu.VMEM((n_seq, WINDOW + tq, kw), BF16),
            pltpu.VMEM((n_seq, kw, WINDOW + tq), BF16),
            pltpu.VMEM((n_seq, WINDOW, kw), BF16),
            pltpu.VMEM((n_seq, kw, WINDOW), BF16),
            pltpu.VMEM((n_seq, qw, tq), BF16),
        ],
        compiler_params=_params("arbitrary", "arbitrary"),
        name="swa",
    )(x, gain.reshape(1, d), w_qkv.T.astype(BF16), qgain_f, kgain_f, cos_f, sin_f,
      band, band0, sink_cols, w_out.astype(BF16))


def kernel(x, mem, positions, mix_norm, hyb_w_in, gla_w_gate2, gla_gate_bias, gla_out_gain, conv_w,
           hyb_w_out, swa_w_qkv, swa_q_gain, swa_k_gain, swa_sinks, swa_w_out, mem_norm, xa_norm, xa_wq,
           xa_wkv, xa_q_gain, xa_k_gain, xa_wo, ffn_norm, ffn_w_gate_up, ffn_w_down):
    b, t, d = x.shape
    depth = mix_norm.shape[0]
    xa_wq_t = jnp.swapaxes(xa_wq, 1, 2).astype(BF16)
    xa_wkv_b, xa_wo_b = xa_wkv.astype(BF16), xa_wo.astype(BF16)
    ffn_wgu_b, ffn_wd_b = ffn_w_gate_up.astype(BF16), ffn_w_down.astype(BF16)
    for layer in range(depth):
        i = layer // 2
        if layer % 2 == 0:
            x = _gla_layer(x, mix_norm[layer], hyb_w_in[i], gla_w_gate2[i], gla_gate_bias[i],
                           gla_out_gain[i], conv_w[i], hyb_w_out[i])
        else:
            x = _swa_layer(x, positions, mix_norm[layer], swa_w_qkv[i], swa_q_gain[i], swa_k_gain[i],
                           swa_sinks[i], swa_w_out[i])
        x = _cross_attention(x, mem, layer, mem_norm, xa_norm, xa_wq_t, xa_wkv_b, xa_q_gain, xa_k_gain,
                             xa_wo_b)
        x = _ffn(x.reshape(b * t, d), layer, ffn_norm, ffn_wgu_b, ffn_wd_b).reshape(b, t, d)
    return x
```

```python
import numpy as np
import jax
import jax.numpy as jnp
from jax import lax
from jax.experimental import pallas as pl
from jax.experimental.pallas import tpu as pltpu

F32 = jnp.float32
BF16 = jnp.bfloat16

D_MODEL = 1024
RMS_EPS = 1e-6
GLA_HEADS = 4
GLA_DK = 64
GLA_DV = 128
GLA_RANK = 16
GLA_TAU = 16.0
GLA_CHUNK = 64
GLA_KW = GLA_HEADS * GLA_DK
GLA_VW = GLA_HEADS * GLA_DV
CONV_DIM = 512
CONV_WIDTH = 3
RANK_PAD = 128
SWA_HEADS = 16
SWA_KV_HEADS = 4
SWA_GROUP = SWA_HEADS // SWA_KV_HEADS
SWA_HEAD_DIM = 64
WINDOW = 128
ROT_DIM = 16
ROPE_THETA = 500000.0
XA_HEADS = 4
XA_HEAD_DIM = 256
FFN_HIDDEN = 2816
FFN_CHUNK = 256

SUBLANES = 8
VMEM_LIMIT = 56 * 1024 * 1024

GLA_TM = 256
GLA_ROWS = 4
SWA_TQ = 512
SWA_ROWS = 2
SWA_WAVE = 8
SWA_LOOKAHEAD = 1
XA_TM = 1024
XA_KV_ROWS = 2
XA_SUB = 2
FFN_TM = 1024
FFN_SUB = 2


def _rms(x, gain):
    return x * lax.rsqrt(jnp.mean(x * x, axis=-1, keepdims=True) + RMS_EPS) * gain


def _dot(a, b):
    return jnp.dot(a, b, preferred_element_type=F32)


def _dot_nt(a, b):
    return lax.dot_general(a, b, (((1,), (1,)), ((), ())), preferred_element_type=F32)


def _dot_tn(a, b):
    return lax.dot_general(a, b, (((0,), (0,)), ((), ())), preferred_element_type=F32)


def _split_bf16(x):
    hi = x.astype(BF16)
    lo = (x - hi.astype(F32)).astype(BF16)
    return hi, lo


def _const_spec(shape):
    return pl.BlockSpec(shape, lambda *_: (0,) * len(shape))


def _layer_spec(shape, layer):
    return pl.BlockSpec((1,) + tuple(shape), lambda *_: (layer,) + (0,) * len(shape))


def _params(*semantics):
    return pltpu.CompilerParams(dimension_semantics=semantics, vmem_limit_bytes=VMEM_LIMIT)


def _ffn_body(x_ref, gain_ref, wgu_ref, wd_ref, o_ref, h_ref, acc_ref):
    ts = x_ref.shape[0] // FFN_SUB
    for i in range(FFN_SUB):
        x = x_ref[i * ts:(i + 1) * ts, :]
        h_ref[i] = _rms(x, gain_ref[0]).astype(BF16)
        acc_ref[i] = x
    for c in range(FFN_HIDDEN // FFN_CHUNK):
        lo = c * FFN_CHUNK
        for i in range(FFN_SUB):
            h = h_ref[i]
            g = _dot(h, wgu_ref[0, :, lo:lo + FFN_CHUNK])
            u = _dot(h, wgu_ref[0, :, FFN_HIDDEN + lo:FFN_HIDDEN + lo + FFN_CHUNK])
            a = (g / (1.0 + jnp.exp(-g)) * u).astype(BF16)
            acc_ref[i] += _dot(a, wd_ref[0, lo:lo + FFN_CHUNK, :])
    for i in range(FFN_SUB):
        o_ref[i * ts:(i + 1) * ts, :] = acc_ref[i]


def _ffn(x2, layer, ffn_norm, w_gate_up, w_down):
    m, d = x2.shape
    tm = min(FFN_TM, m)
    return pl.pallas_call(
        _ffn_body,
        out_shape=jax.ShapeDtypeStruct((m, d), F32),
        grid=(m // tm,),
        in_specs=[
            pl.BlockSpec((tm, d), lambda i: (i, 0)),
            _layer_spec((1, d), layer),
            _layer_spec((d, 2 * FFN_HIDDEN), layer),
            _layer_spec((FFN_HIDDEN, d), layer),
        ],
        out_specs=pl.BlockSpec((tm, d), lambda i: (i, 0)),
        scratch_shapes=[pltpu.VMEM((FFN_SUB, tm // FFN_SUB, d), BF16),
                        pltpu.VMEM((FFN_SUB, tm // FFN_SUB, d), F32)],
        compiler_params=_params("parallel"),
        name="ffn",
    )(x2, ffn_norm[:, None, :], w_gate_up, w_down)


def _xa_kv_body(mem_ref, gain_ref, wkv_ref, kgain_ref, k_ref, vt_ref):
    rows, mem = mem_ref.shape[0], mem_ref.shape[1]
    mh = jnp.concatenate([_rms(mem_ref[r], gain_ref[0]).astype(BF16) for r in range(rows)], axis=0)
    kv = _dot(mh, wkv_ref[0])
    for r in range(rows):
        kv_r = kv[r * mem:(r + 1) * mem]
        for h in range(XA_HEADS):
            lo = h * XA_HEAD_DIM
            k_ref[r, :, lo:lo + XA_HEAD_DIM] = _rms(kv_r[:, lo:lo + XA_HEAD_DIM], kgain_ref[0]).astype(BF16)
        vt_ref[r] = kv_r[:, D_MODEL:].T.astype(BF16)


def _xa_body(x_ref, gain_ref, wq_t_ref, qgain_ref, k_ref, vt_ref, wo_ref, o_ref, att_ref):
    hd = XA_HEAD_DIM
    mem = k_ref.shape[1]
    ts = x_ref.shape[1] // XA_SUB
    subs = range(XA_SUB)
    xs = [x_ref[0, i * ts:(i + 1) * ts, :] for i in subs]
    q_t = [_dot_nt(wq_t_ref[0], _rms(x, gain_ref[0]).astype(BF16)) for x in xs]
    qscale = hd ** -0.5 * float(np.log2(np.e))
    ones_rows = jnp.ones((2 * SUBLANES, mem), BF16)

    units = [(i, head) for i in subs for head in range(XA_HEADS)]
    scores = []
    for i, head in units:
        z = q_t[i][head * hd:(head + 1) * hd, :]
        ss = jnp.sum(z * z, axis=0, keepdims=True)
        z = z * (lax.rsqrt(ss * (1.0 / hd) + RMS_EPS) * qscale) * qgain_ref[0]
        scores.append(_dot(k_ref[0, :, head * hd:(head + 1) * hd], z.astype(BF16)))

    for i in subs:
        sub_units = [(u, unit) for u, unit in enumerate(units) if unit[0] == i]
        probs = [jnp.exp2(scores[u] - jnp.max(scores[u], axis=0, keepdims=True)).astype(BF16)
                 for u, _ in sub_units]
        for (u, (_, head)), p in zip(sub_units, probs):
            v_aug = jnp.concatenate([vt_ref[0, head * hd:(head + 1) * hd, :], ones_rows], axis=0)
            o_aug = _dot(v_aug, p)
            o = o_aug[0:hd, :] * (1.0 / o_aug[hd:hd + 1, :])
            att_ref[i, head * hd:(head + 1) * hd, :] = o.astype(BF16)
        o_ref[0, i * ts:(i + 1) * ts, :] = xs[i] + _dot_tn(att_ref[i], wo_ref[0])


def _cross_attention(x, mem, layer, mem_norm, xa_norm, wq_t, wkv, q_gain, k_gain, wo):
    b, t, d = x.shape
    m = mem.shape[1]
    k, vt = pl.pallas_call(
        _xa_kv_body,
        out_shape=(jax.ShapeDtypeStruct((b, m, d), BF16), jax.ShapeDtypeStruct((b, d, m), BF16)),
        grid=(b // XA_KV_ROWS,),
        in_specs=[
            pl.BlockSpec((XA_KV_ROWS, m, d), lambda i: (i, 0, 0)),
            _layer_spec((1, d), layer),
            _layer_spec((d, 2 * d), layer),
            _layer_spec((1, XA_HEAD_DIM), layer),
        ],
        out_specs=(pl.BlockSpec((XA_KV_ROWS, m, d), lambda i: (i, 0, 0)),
                   pl.BlockSpec((XA_KV_ROWS, d, m), lambda i: (i, 0, 0))),
        compiler_params=_params("parallel"),
        name="xa_kv",
    )(mem, mem_norm[:, None, :], wkv, k_gain[:, None, :])
    tm = min(XA_TM, t)
    return pl.pallas_call(
        _xa_body,
        out_shape=jax.ShapeDtypeStruct((b, t, d), F32),
        grid=(b, t // tm),
        in_specs=[
            pl.BlockSpec((1, tm, d), lambda i, j: (i, j, 0)),
            _layer_spec((1, d), layer),
            _layer_spec((d, d), layer),
            _layer_spec((XA_HEAD_DIM, 1), layer),
            pl.BlockSpec((1, m, d), lambda i, j: (i, 0, 0)),
            pl.BlockSpec((1, d, m), lambda i, j: (i, 0, 0)),
            _layer_spec((d, d), layer),
        ],
        out_specs=pl.BlockSpec((1, tm, d), lambda i, j: (i, j, 0)),
        scratch_shapes=[pltpu.VMEM((XA_SUB, d, tm // XA_SUB), BF16)],
        compiler_params=_params("parallel", "parallel"),
        name="xa",
    )(x, xa_norm[:, None, :], wq_t, q_gain[:, :, None], k, vt, wo)


_GLA_Q, _GLA_K, _GLA_V, _GLA_GO, _GLA_CB, _GLA_CC, _GLA_CI, _GLA_GL = (
    0, 256, 512, 1024, 1536, 2048, 2560, 3072)
GLA_PROJ_W = _GLA_GL + RANK_PAD


def _gla_body(x_ref, gain_ref, win_ref, wg2_ref, gbias_ref, ogain_ref, convw_ref, wout_ref,
              tri_ref, hmask_ref, smask_ref, o_ref,
              st_ref, ubuf_ref, utail_ref, cat_ref):
    rows, tm = x_ref.shape[0], x_ref.shape[1]
    seqs = range(rows)

    @pl.when(pl.program_id(1) == 0)
    def _():
        st_ref[...] = jnp.zeros_like(st_ref)
        utail_ref[...] = jnp.zeros_like(utail_ref)

    xs, projs, bs, b_lasts = [], [], [], []
    q_in, k_in, k_out, decay, v = [], [], [], [], []

    def project(r):
        xs.append(x_ref[r])
        projs.append(_dot(_rms(xs[r], gain_ref[...]).astype(BF16), win_ref[...]))

    def gates(r):
        logits = _dot(projs[r][:, _GLA_GL:_GLA_GL + RANK_PAD].astype(BF16), wg2_ref[...]) + gbias_ref[...]
        log_a = (jnp.minimum(logits, 0.0) - jnp.log(1.0 + jnp.exp(-jnp.abs(logits)))) * (1.0 / GLA_TAU)
        cum = _dot(tri_ref[...], jnp.concatenate(_split_bf16(log_a), axis=1))
        b = cum[:, :GLA_KW] + cum[:, GLA_KW:]
        bs.append(b)
        b_lasts.append(jnp.concatenate(
            [jnp.broadcast_to(b[c + GLA_CHUNK - 1:c + GLA_CHUNK, :], (GLA_CHUNK, GLA_KW))
             for c in range(0, tm, GLA_CHUNK)], axis=0))

    def vector_stage(r):
        proj, b, b_last = projs[r], bs[r], b_lasts[r]
        q = proj[:, _GLA_Q:_GLA_Q + GLA_KW] * (GLA_DK ** -0.5)
        k = proj[:, _GLA_K:_GLA_K + GLA_KW]
        v.append(proj[:, _GLA_V:_GLA_V + GLA_VW].astype(BF16))
        q_in.append((q * jnp.exp(b)).astype(BF16))
        k_in.append((k * jnp.exp(-b)).astype(BF16))
        k_out.append((k * jnp.exp(b_last - b)).astype(BF16))
        decay.append(jnp.exp(b_last))
        u = proj[:, _GLA_CC:_GLA_CC + CONV_DIM] * proj[:, _GLA_CI:_GLA_CI + CONV_DIM]
        ubuf_ref[r, 0:SUBLANES, :] = utail_ref[r]
        ubuf_ref[r, SUBLANES:SUBLANES + tm, :] = u
        y = (convw_ref[0:1, :] * ubuf_ref[r, SUBLANES - 2:SUBLANES - 2 + tm, :]
             + convw_ref[1:2, :] * ubuf_ref[r, SUBLANES - 1:SUBLANES - 1 + tm, :]
             + convw_ref[2:3, :] * u)
        cat_ref[r, :, GLA_VW:GLA_VW + CONV_DIM] = (proj[:, _GLA_CB:_GLA_CB + CONV_DIM] * y).astype(BF16)

    project(0)
    for r in seqs:
        if r + 1 < rows:
            project(r + 1)
        gates(r)
        vector_stage(r)

    o_inter = [[] for _ in seqs]
    for c in range(tm // GLA_CHUNK):
        r0 = c * GLA_CHUNK
        for r in seqs:
            st = st_ref[r]
            o_inter[r].append(_dot_nt(q_in[r][r0:r0 + GLA_CHUNK], st.astype(BF16)))
            v_c, k_c = v[r][r0:r0 + GLA_CHUNK], k_out[r][r0:r0 + GLA_CHUNK]
            half = GLA_VW // 2
            kv_t = jnp.concatenate([_dot_tn(v_c[:, :half], k_c), _dot_tn(v_c[:, half:], k_c)], axis=0)
            st_ref[r] = st * decay[r][r0:r0 + 1, :] + jnp.where(smask_ref[...] > 0.5, kv_t, 0.0)
    o_inter = [jnp.concatenate(o, axis=0) for o in o_inter]

    causal = tri_ref[...] > 0.5
    pair_w = 2 * GLA_DV
    first_head = lax.broadcasted_iota(jnp.int32, (1, pair_w), 1) < GLA_DV
    for pair in range(GLA_HEADS // 2):
        lo = pair * pair_w
        for r in seqs:
            scores = []
            for hd in (2 * pair, 2 * pair + 1):
                q_h = jnp.where(hmask_ref[hd:hd + 1, :] > 0.5, q_in[r], jnp.zeros_like(q_in[r]))
                scores.append(jnp.where(causal, _dot_nt(q_h, k_in[r]), 0.0).astype(BF16))
            v_pair = v[r][:, lo:lo + pair_w]
            v_diag = jnp.concatenate([jnp.where(first_head, v_pair, jnp.zeros_like(v_pair)),
                                      jnp.where(first_head, jnp.zeros_like(v_pair), v_pair)], axis=0)
            o_pair = _dot(jnp.concatenate(scores, axis=1), v_diag) + o_inter[r][:, lo:lo + pair_w]
            for hd in (2 * pair, 2 * pair + 1):
                off = (hd - 2 * pair) * GLA_DV
                g_h = projs[r][:, _GLA_GO + hd * GLA_DV:_GLA_GO + (hd + 1) * GLA_DV]
                y_h = _rms(o_pair[:, off:off + GLA_DV], ogain_ref[...]) * (g_h / (1.0 + jnp.exp(-g_h)))
                cat_ref[r, :, hd * GLA_DV:(hd + 1) * GLA_DV] = y_h.astype(BF16)

    for r in seqs:
        o_ref[r] = xs[r] + _dot(cat_ref[r], wout_ref[...])

    @pl.when(pl.program_id(1) + 1 < pl.num_programs(1))
    def _():
        for r in seqs:
            utail_ref[r] = ubuf_ref[r, tm:tm + SUBLANES, :]


def _gla_layer(x, gain, w_in, w_gate2, gate_bias, out_gain, conv_w, w_out):
    b, t, d = x.shape
    tm = min(GLA_TM, t)
    rows = GLA_ROWS
    widths = [GLA_KW, GLA_KW, GLA_VW, GLA_RANK, GLA_VW, CONV_DIM, CONV_DIM, CONV_DIM]
    cuts = [int(c) for c in np.cumsum(widths)[:-1]]
    wq, wk, wv, wgl, wgo, wcb, wcc, wci = jnp.split(w_in, cuts, axis=1)
    wgl = jnp.pad(wgl, ((0, 0), (0, RANK_PAD - GLA_RANK)))
    w_in_r = jnp.concatenate([wq, wk, wv, wgo, wcb, wcc, wci, wgl], axis=1).astype(BF16)
    wg2 = jnp.pad(w_gate2, ((0, RANK_PAD - GLA_RANK), (0, 0))).astype(BF16)

    r = np.arange(tm)
    same_chunk = (r[:, None] // GLA_CHUNK) == (r[None, :] // GLA_CHUNK)
    tri = jnp.asarray(same_chunk & (r[None, :] <= r[:, None]), BF16)
    hmask = jnp.asarray(np.arange(GLA_KW)[None, :] // GLA_DK == np.arange(GLA_HEADS)[:, None], F32)
    smask = jnp.asarray(
        (np.arange(GLA_VW)[:, None] // GLA_DV) == (np.arange(GLA_KW)[None, :] // GLA_DK), F32)

    return pl.pallas_call(
        _gla_body,
        out_shape=jax.ShapeDtypeStruct((b, t, d), F32),
        grid=(b // rows, t // tm),
        in_specs=[
            pl.BlockSpec((rows, tm, d), lambda i, j: (i, j, 0)),
            _const_spec((1, d)),
            _const_spec((d, GLA_PROJ_W)),
            _const_spec((RANK_PAD, GLA_KW)),
            _const_spec((1, GLA_KW)),
            _const_spec((1, GLA_DV)),
            _const_spec((CONV_WIDTH, CONV_DIM)),
            _const_spec((GLA_VW + CONV_DIM, d)),
            _const_spec((tm, tm)),
            _const_spec((GLA_HEADS, GLA_KW)),
            _const_spec((GLA_VW, GLA_KW)),
        ],
        out_specs=pl.BlockSpec((rows, tm, d), lambda i, j: (i, j, 0)),
        scratch_shapes=[
            pltpu.VMEM((rows, GLA_VW, GLA_KW), F32),
            pltpu.VMEM((rows, tm + SUBLANES, CONV_DIM), F32),
            pltpu.VMEM((rows, SUBLANES, CONV_DIM), F32),
            pltpu.VMEM((rows, tm, GLA_VW + CONV_DIM), BF16),
        ],
        compiler_params=_params("parallel", "arbitrary"),
        name="gla_conv",
    )(x, gain.reshape(1, d), w_in_r, wg2, gate_bias.reshape(1, GLA_KW), out_gain.reshape(1, GLA_DV),
      conv_w, w_out.astype(BF16), tri, hmask, smask)


def _norm_rope_heads(z, n_heads, gain, cos_f, sin_f):
    hd, half = SWA_HEAD_DIM, ROT_DIM // 2
    out = []
    for head in range(n_heads):
        zh = z[head * hd:(head + 1) * hd, :]
        ss = jnp.sum(zh * zh, axis=0, keepdims=True)
        zh = zh * lax.rsqrt(ss * (1.0 / hd) + RMS_EPS) * gain
        x1, x2 = zh[0:half], zh[half:ROT_DIM]
        out.append(jnp.concatenate(
            [x1 * cos_f - x2 * sin_f, x2 * cos_f + x1 * sin_f, zh[ROT_DIM:]], axis=0))
    return out


def _swa_body(x_ref, gain_ref, wqkv_t_ref, qgain_ref, kgain_ref, cos_ref, sin_ref,
              band_ref, band0_ref, sink_ref, wout_ref, o_ref,
              qpad_ref, kbuf_ref, vbuf_ref, kprev_ref, vprev_ref, att_ref):
    n_seq, tq = x_ref.shape[0], x_ref.shape[1]
    seqs = range(n_seq)
    nb = tq // WINDOW
    n = pl.program_id(1)
    hd = SWA_HEAD_DIM
    qw = SWA_HEADS * hd
    kw = SWA_KV_HEADS * hd

    @pl.when((pl.program_id(0) == 0) & (n == 0))
    def _():
        qpad_ref[...] = jnp.zeros_like(qpad_ref)

    @pl.when(n == 0)
    def _():
        kprev_ref[...] = jnp.zeros_like(kprev_ref)
        vprev_ref[...] = jnp.zeros_like(vprev_ref)

    xs = [x_ref[r] for r in seqs]
    qkv_t = [_dot_nt(wqkv_t_ref[...], _rms(x, gain_ref[...]).astype(BF16)) for x in xs]

    for r in seqs:
        cos_f, sin_f = cos_ref[r], sin_ref[r]
        k_heads = _norm_rope_heads(qkv_t[r][qw:qw + kw], SWA_KV_HEADS, kgain_ref[...], cos_f, sin_f)
        kbuf_ref[r, 0:WINDOW, :] = kprev_ref[r]
        kbuf_ref[r, WINDOW:WINDOW + tq, :] = jnp.concatenate(k_heads, axis=0).T.astype(BF16)
        vbuf_ref[r, :, 0:WINDOW] = vprev_ref[r]
        vbuf_ref[r, :, WINDOW:WINDOW + tq] = qkv_t[r][qw + kw:, :].astype(BF16)
        q_heads = _norm_rope_heads(qkv_t[r][:qw], SWA_HEADS, qgain_ref[...], cos_f, sin_f)
        for head, z in enumerate(q_heads):
            g, j = divmod(head, SWA_GROUP)
            z = z.astype(BF16)
            for i in range(nb):
                qpad_ref[r, g, i, g * hd:(g + 1) * hd, j * WINDOW:(j + 1) * WINDOW] = (
                    z[:, i * WINDOW:(i + 1) * WINDOW])

    bias0 = jnp.where(n > 0, band_ref[...], band0_ref[...])

    units = [(r, g, i) for g in range(SWA_KV_HEADS) for i in range(nb) for r in seqs]

    def scores(unit):
        r, g, i = unit
        return _dot(kbuf_ref[r, i * WINDOW:(i + 2) * WINDOW, :], qpad_ref[r, g, i])

    ones_rows = jnp.ones((2 * SUBLANES, 2 * WINDOW), BF16)
    waves = [units[w:w + SWA_WAVE] for w in range(0, len(units), SWA_WAVE)]
    pending = [[scores(unit) for unit in wave] for wave in waves[:SWA_LOOKAHEAD]]
    for w, wave in enumerate(waves):
        s_wave = pending.pop(0)
        if w + SWA_LOOKAHEAD < len(waves):
            pending.append([scores(unit) for unit in waves[w + SWA_LOOKAHEAD]])
        stats, probs = [], []
        for (r, g, i), s in zip(wave, s_wave):
            s = s + (bias0 if i == 0 else band_ref[...])
            sink = sink_ref[g]
            m = jnp.maximum(jnp.max(s, axis=0, keepdims=True), sink)
            if w + SWA_LOOKAHEAD < len(waves):
                m = m + jnp.minimum(jnp.abs(pending[-1][0][0:1, :]), 0.0)
            probs.append(jnp.exp2(s - m).astype(BF16))
            stats.append((sink, m))
        for (r, g, i), p, (sink, m) in zip(wave, probs, stats):
            r0 = i * WINDOW
            v_aug = jnp.concatenate(
                [vbuf_ref[r, g * hd:(g + 1) * hd, r0:r0 + 2 * WINDOW], ones_rows], axis=0)
            o_aug = _dot(v_aug, p)
            denom = o_aug[hd:hd + 1, :] + jnp.exp2(sink - m)
            o = o_aug[0:hd, :] * (1.0 / denom)
            for j in range(SWA_GROUP):
                head = g * SWA_GROUP + j
                att_ref[r, head * hd:(head + 1) * hd, r0:r0 + WINDOW] = (
                    o[:, j * WINDOW:(j + 1) * WINDOW].astype(BF16))

    for r in seqs:
        o_ref[r] = xs[r] + _dot_tn(att_ref[r], wout_ref[...])

    @pl.when(n + 1 < pl.num_programs(1))
    def _():
        for r in seqs:
            kprev_ref[r] = kbuf_ref[r, tq:tq + WINDOW, :]
            vprev_ref[r] = vbuf_ref[r, :, tq:tq + WINDOW]


def _swa_layer(x, positions, gain, w_qkv, q_gain, k_gain, sinks, w_out):
    b, t, d = x.shape
    tq = min(SWA_TQ, t)
    nb = tq // WINDOW
    hd = SWA_HEAD_DIM
    qw = SWA_HEADS * hd
    kw = SWA_KV_HEADS * hd
    inv_freq = ROPE_THETA ** (-jnp.arange(0, ROT_DIM, 2, dtype=F32) / ROT_DIM)
    ang = positions.astype(F32)[:, None, :] * inv_freq[None, :, None]
    cos_f, sin_f = jnp.cos(ang), jnp.sin(ang)
    log2e = float(np.log2(np.e))
    qgain_f = jnp.broadcast_to((q_gain * (hd ** -0.5 * log2e))[:, None], (hd, tq))
    kgain_f = jnp.broadcast_to(k_gain[:, None], (hd, tq))
    rows = SWA_GROUP * WINDOW
    si = np.arange(2 * WINDOW)[:, None]
    qi = np.arange(WINDOW)[None, :] + WINDOW
    band_ok = (qi - si >= 0) & (qi - si < WINDOW)
    band = jnp.asarray(np.tile(np.where(band_ok, 0.0, -np.inf), (1, SWA_GROUP)), F32)
    band0 = jnp.asarray(np.tile(np.where(band_ok & (si >= WINDOW), 0.0, -np.inf), (1, SWA_GROUP)), F32)
    sink_cols = jnp.repeat((sinks.astype(F32) * log2e).reshape(SWA_KV_HEADS, 1, SWA_GROUP), WINDOW, axis=2)

    n_seq = SWA_ROWS
    row_spec = pl.BlockSpec((n_seq, tq, d), lambda i, j: (i, j, 0))
    tab_spec = pl.BlockSpec((n_seq, ROT_DIM // 2, tq), lambda i, j: (i, 0, j))
    return pl.pallas_call(
        _swa_body,
        out_shape=jax.ShapeDtypeStruct((b, t, d), F32),
        grid=(b // n_seq, t // tq),
        in_specs=[
            row_spec,
            _const_spec((1, d)),
            _const_spec((qw + 2 * kw, d)),
            _const_spec((hd, tq)),
            _const_spec((hd, tq)),
            tab_spec, tab_spec,
            _const_spec((2 * WINDOW, rows)),
            _const_spec((2 * WINDOW, rows)),
            _const_spec((SWA_KV_HEADS, 1, rows)),
            _const_spec((d, d)),
        ],
        out_specs=row_spec,
        scratch_shapes=[
            pltpu.VMEM((n_seq, SWA_KV_HEADS, nb, kw, rows), BF16),
            pltpu.VMEM((n_seq, WINDOW + tq, kw), BF16),
            pltpu.VMEM((n_seq, kw, WINDOW + tq), BF16),
            pltpu.VMEM((n_seq, WINDOW, kw), BF16),
            pltpu.VMEM((n_seq, kw, WINDOW), BF16),
            pltpu.VMEM((n_seq, qw, tq), BF16),
        ],
        compiler_params=_params("arbitrary", "arbitrary"),
        name="swa",
    )(x, gain.reshape(1, d), w_qkv.T.astype(BF16), qgain_f, kgain_f, cos_f, sin_f,
      band, band0, sink_cols, w_out.astype(BF16))


def kernel(x, mem, positions, mix_norm, hyb_w_in, gla_w_gate2, gla_gate_bias, gla_out_gain, conv_w,
           hyb_w_out, swa_w_qkv, swa_q_gain, swa_k_gain, swa_sinks, swa_w_out, mem_norm, xa_norm, xa_wq,
           xa_wkv, xa_q_gain, xa_k_gain, xa_wo, ffn_norm, ffn_w_gate_up, ffn_w_down):
    b, t, d = x.shape
    depth = mix_norm.shape[0]
    xa_wq_t = jnp.swapaxes(xa_wq, 1, 2).astype(BF16)
    xa_wkv_b, xa_wo_b = xa_wkv.astype(BF16), xa_wo.astype(BF16)
    ffn_wgu_b, ffn_wd_b = ffn_w_gate_up.astype(BF16), ffn_w_down.astype(BF16)
    for layer in range(depth):
        i = layer // 2
        if layer % 2 == 0:
            x = _gla_layer(x, mix_norm[layer], hyb_w_in[i], gla_w_gate2[i], gla_gate_bias[i],
                           gla_out_gain[i], conv_w[i], hyb_w_out[i])
        else:
            x = _swa_layer(x, positions, mix_norm[layer], swa_w_qkv[i], swa_q_gain[i], swa_k_gain[i],
                           swa_sinks[i], swa_w_out[i])
        x = _cross_attention(x, mem, layer, mem_norm, xa_norm, xa_wq_t, xa_wkv_b, xa_q_gain, xa_k_gain,
                             xa_wo_b)
        x = _ffn(x.reshape(b * t, d), layer, ffn_norm, ffn_wgu_b, ffn_wd_b).reshape(b, t, d)
    return x
```

```python
import numpy as np
import jax
import jax.numpy as jnp
from jax import lax
from jax.experimental import pallas as pl
from jax.experimental.pallas import tpu as pltpu

F32 = jnp.float32
BF16 = jnp.bfloat16

D_MODEL = 1024
RMS_EPS = 1e-6
GLA_HEADS = 4
GLA_DK = 64
GLA_DV = 128
GLA_RANK = 16
GLA_TAU = 16.0
GLA_CHUNK = 64
GLA_KW = GLA_HEADS * GLA_DK
GLA_VW = GLA_HEADS * GLA_DV
CONV_DIM = 512
CONV_WIDTH = 3
RANK_PAD = 128
SWA_HEADS = 16
SWA_KV_HEADS = 4
SWA_GROUP = SWA_HEADS // SWA_KV_HEADS
SWA_HEAD_DIM = 64
WINDOW = 128
ROT_DIM = 16
ROPE_THETA = 500000.0
XA_HEADS = 4
XA_HEAD_DIM = 256
FFN_HIDDEN = 2816
FFN_CHUNK = 256

SUBLANES = 8
VMEM_LIMIT = 56 * 1024 * 1024

GLA_TM = 256
GLA_ROWS = 4
SWA_TQ = 512
SWA_ROWS = 2
SWA_WAVE = 8
SWA_LOOKAHEAD = 1
XA_TM = 1024
XA_KV_ROWS = 2
XA_SUB = 2
FFN_TM = 1024
FFN_SUB = 2


def _rms(x, gain):
    return x * lax.rsqrt(jnp.mean(x * x, axis=-1, keepdims=True) + RMS_EPS) * gain


def _dot(a, b):
    return jnp.dot(a, b, preferred_element_type=F32)


def _dot_nt(a, b):
    return lax.dot_general(a, b, (((1,), (1,)), ((), ())), preferred_element_type=F32)


def _dot_tn(a, b):
    return lax.dot_general(a, b, (((0,), (0,)), ((), ())), preferred_element_type=F32)


def _split_bf16(x):
    hi = x.astype(BF16)
    lo = (x - hi.astype(F32)).astype(BF16)
    return hi, lo


def _const_spec(shape):
    return pl.BlockSpec(shape, lambda *_: (0,) * len(shape))


def _layer_spec(shape, layer):
    return pl.BlockSpec((1,) + tuple(shape), lambda *_: (layer,) + (0,) * len(shape))


def _params(*semantics):
    return pltpu.CompilerParams(dimension_semantics=semantics, vmem_limit_bytes=VMEM_LIMIT)


def _ffn_body(x_ref, gain_ref, wgu_ref, wd_ref, o_ref, h_ref, acc_ref):
    ts = x_ref.shape[0] // FFN_SUB
    for i in range(FFN_SUB):
        x = x_ref[i * ts:(i + 1) * ts, :]
        h_ref[i] = _rms(x, gain_ref[0]).astype(BF16)
        acc_ref[i] = x
    for c in range(FFN_HIDDEN // FFN_CHUNK):
        lo = c * FFN_CHUNK
        for i in range(FFN_SUB):
            h = h_ref[i]
            g = _dot(h, wgu_ref[0, :, lo:lo + FFN_CHUNK])
            u = _dot(h, wgu_ref[0, :, FFN_HIDDEN + lo:FFN_HIDDEN + lo + FFN_CHUNK])
            a = (g / (1.0 + jnp.exp(-g)) * u).astype(BF16)
            acc_ref[i] += _dot(a, wd_ref[0, lo:lo + FFN_CHUNK, :])
    for i in range(FFN_SUB):
        o_ref[i * ts:(i + 1) * ts, :] = acc_ref[i]


def _ffn(x2, layer, ffn_norm, w_gate_up, w_down):
    m, d = x2.shape
    tm = min(FFN_TM, m)
    return pl.pallas_call(
        _ffn_body,
        out_shape=jax.ShapeDtypeStruct((m, d), F32),
        grid=(m // tm,),
        in_specs=[
            pl.BlockSpec((tm, d), lambda i: (i, 0)),
            _layer_spec((1, d), layer),
            _layer_spec((d, 2 * FFN_HIDDEN), layer),
            _layer_spec((FFN_HIDDEN, d), layer),
        ],
        out_specs=pl.BlockSpec((tm, d), lambda i: (i, 0)),
        scratch_shapes=[pltpu.VMEM((FFN_SUB, tm // FFN_SUB, d), BF16),
                        pltpu.VMEM((FFN_SUB, tm // FFN_SUB, d), F32)],
        compiler_params=_params("parallel"),
        name="ffn",
    )(x2, ffn_norm[:, None, :], w_gate_up, w_down)


def _xa_kv_body(mem_ref, gain_ref, wkv_ref, kgain_ref, k_ref, vt_ref):
    rows, mem = mem_ref.shape[0], mem_ref.shape[1]
    mh = jnp.concatenate([_rms(mem_ref[r], gain_ref[0]).astype(BF16) for r in range(rows)], axis=0)
    kv = _dot(mh, wkv_ref[0])
    for r in range(rows):
        kv_r = kv[r * mem:(r + 1) * mem]
        for h in range(XA_HEADS):
            lo = h * XA_HEAD_DIM
            k_ref[r, :, lo:lo + XA_HEAD_DIM] = _rms(kv_r[:, lo:lo + XA_HEAD_DIM], kgain_ref[0]).astype(BF16)
        vt_ref[r] = kv_r[:, D_MODEL:].T.astype(BF16)


def _xa_body(x_ref, gain_ref, wq_t_ref, qgain_ref, k_ref, vt_ref, wo_ref, o_ref, att_ref):
    hd = XA_HEAD_DIM
    mem = k_ref.shape[1]
    ts = x_ref.shape[1] // XA_SUB
    subs = range(XA_SUB)
    xs = [x_ref[0, i * ts:(i + 1) * ts, :] for i in subs]
    q_t = [_dot_nt(wq_t_ref[0], _rms(x, gain_ref[0]).astype(BF16)) for x in xs]
    qscale = hd ** -0.5 * float(np.log2(np.e))
    ones_rows = jnp.ones((2 * SUBLANES, mem), BF16)

    units = [(i, head) for i in subs for head in range(XA_HEADS)]
    scores = []
    for i, head in units:
        z = q_t[i][head * hd:(head + 1) * hd, :]
        ss = jnp.sum(z * z, axis=0, keepdims=True)
        z = z * (lax.rsqrt(ss * (1.0 / hd) + RMS_EPS) * qscale) * qgain_ref[0]
        scores.append(_dot(k_ref[0, :, head * hd:(head + 1) * hd], z.astype(BF16)))

    for i in subs:
        sub_units = [(u, unit) for u, unit in enumerate(units) if unit[0] == i]
        probs = [jnp.exp2(scores[u] - jnp.max(scores[u], axis=0, keepdims=True)).astype(BF16)
                 for u, _ in sub_units]
        for (u, (_, head)), p in zip(sub_units, probs):
            v_aug = jnp.concatenate([vt_ref[0, head * hd:(head + 1) * hd, :], ones_rows], axis=0)
            o_aug = _dot(v_aug, p)
            o = o_aug[0:hd, :] * (1.0 / o_aug[hd:hd + 1, :])
            att_ref[i, head * hd:(head + 1) * hd, :] = o.astype(BF16)
        o_ref[0, i * ts:(i + 1) * ts, :] = xs[i] + _dot_tn(att_ref[i], wo_ref[0])


def _cross_attention(x, mem, layer, mem_norm, xa_norm, wq_t, wkv, q_gain, k_gain, wo):
    b, t, d = x.shape
    m = mem.shape[1]
    k, vt = pl.pallas_call(
        _xa_kv_body,
        out_shape=(jax.ShapeDtypeStruct((b, m, d), BF16), jax.ShapeDtypeStruct((b, d, m), BF16)),
        grid=(b // XA_KV_ROWS,),
        in_specs=[
            pl.BlockSpec((XA_KV_ROWS, m, d), lambda i: (i, 0, 0)),
            _layer_spec((1, d), layer),
            _layer_spec((d, 2 * d), layer),
            _layer_spec((1, XA_HEAD_DIM), layer),
        ],
        out_specs=(pl.BlockSpec((XA_KV_ROWS, m, d), lambda i: (i, 0, 0)),
                   pl.BlockSpec((XA_KV_ROWS, d, m), lambda i: (i, 0, 0))),
        compiler_params=_params("parallel"),
        name="xa_kv",
    )(mem, mem_norm[:, None, :], wkv, k_gain[:, None, :])
    tm = min(XA_TM, t)
    return pl.pallas_call(
        _xa_body,
        out_shape=jax.ShapeDtypeStruct((b, t, d), F32),
        grid=(b, t // tm),
        in_specs=[
            pl.BlockSpec((1, tm, d), lambda i, j: (i, j, 0)),
            _layer_spec((1, d), layer),
            _layer_spec((d, d), layer),
            _layer_spec((XA_HEAD_DIM, 1), layer),
            pl.BlockSpec((1, m, d), lambda i, j: (i, 0, 0)),
            pl.BlockSpec((1, d, m), lambda i, j: (i, 0, 0)),
            _layer_spec((d, d), layer),
        ],
        out_specs=pl.BlockSpec((1, tm, d), lambda i, j: (i, j, 0)),
        scratch_shapes=[pltpu.VMEM((XA_SUB, d, tm // XA_SUB), BF16)],
        compiler_params=_params("parallel", "parallel"),
        name="xa",
    )(x, xa_norm[:, None, :], wq_t, q_gain[:, :, None], k, vt, wo)


_GLA_Q, _GLA_K, _GLA_V, _GLA_GO, _GLA_CB, _GLA_CC, _GLA_CI, _GLA_GL = (
    0, 256, 512, 1024, 1536, 2048, 2560, 3072)
GLA_PROJ_W = _GLA_GL + RANK_PAD


def _gla_body(x_ref, gain_ref, win_ref, wg2_ref, gbias_ref, ogain_ref, convw_ref, wout_ref,
              tri_ref, hmask_ref, smask_ref, o_ref,
              st_ref, ubuf_ref, utail_ref, cat_ref):
    rows, tm = x_ref.shape[0], x_ref.shape[1]
    seqs = range(rows)

    @pl.when(pl.program_id(1) == 0)
    def _():
        st_ref[...] = jnp.zeros_like(st_ref)
        utail_ref[...] = jnp.zeros_like(utail_ref)

    xs, projs, bs, b_lasts = [], [], [], []
    q_in, k_in, k_out, decay, v = [], [], [], [], []

    def project(r):
        xs.append(x_ref[r])
        projs.append(_dot(_rms(xs[r], gain_ref[...]).astype(BF16), win_ref[...]))

    def gates(r):
        logits = _dot(projs[r][:, _GLA_GL:_GLA_GL + RANK_PAD].astype(BF16), wg2_ref[...]) + gbias_ref[...]
        log_a = (jnp.minimum(logits, 0.0) - jnp.log(1.0 + jnp.exp(-jnp.abs(logits)))) * (1.0 / GLA_TAU)
        cum = _dot(tri_ref[...], jnp.concatenate(_split_bf16(log_a), axis=1))
        b = cum[:, :GLA_KW] + cum[:, GLA_KW:]
        bs.append(b)
        b_lasts.append(jnp.concatenate(
            [jnp.broadcast_to(b[c + GLA_CHUNK - 1:c + GLA_CHUNK, :], (GLA_CHUNK, GLA_KW))
             for c in range(0, tm, GLA_CHUNK)], axis=0))

    def vector_stage(r):
        proj, b, b_last = projs[r], bs[r], b_lasts[r]
        q = proj[:, _GLA_Q:_GLA_Q + GLA_KW] * (GLA_DK ** -0.5)
        k = proj[:, _GLA_K:_GLA_K + GLA_KW]
        v.append(proj[:, _GLA_V:_GLA_V + GLA_VW].astype(BF16))
        q_in.append((q * jnp.exp(b)).astype(BF16))
        k_in.append((k * jnp.exp(-b)).astype(BF16))
        k_out.append((k * jnp.exp(b_last - b)).astype(BF16))
        decay.append(jnp.exp(b_last))
        u = proj[:, _GLA_CC:_GLA_CC + CONV_DIM] * proj[:, _GLA_CI:_GLA_CI + CONV_DIM]
        ubuf_ref[r, 0:SUBLANES, :] = utail_ref[r]
        ubuf_ref[r, SUBLANES:SUBLANES + tm, :] = u
        y = (convw_ref[0:1, :] * ubuf_ref[r, SUBLANES - 2:SUBLANES - 2 + tm, :]
             + convw_ref[1:2, :] * ubuf_ref[r, SUBLANES - 1:SUBLANES - 1 + tm, :]
             + convw_ref[2:3, :] * u)
        cat_ref[r, :, GLA_VW:GLA_VW + CONV_DIM] = (proj[:, _GLA_CB:_GLA_CB + CONV_DIM] * y).astype(BF16)

    project(0)
    for r in seqs:
        if r + 1 < rows:
            project(r + 1)
        gates(r)
        vector_stage(r)

    o_inter = [[] for _ in seqs]
    for c in range(tm // GLA_CHUNK):
        r0 = c * GLA_CHUNK
        for r in seqs:
            st = st_ref[r]
            o_inter[r].append(_dot_nt(q_in[r][r0:r0 + GLA_CHUNK], st.astype(BF16)))
            v_c, k_c = v[r][r0:r0 + GLA_CHUNK], k_out[r][r0:r0 + GLA_CHUNK]
            half = GLA_VW // 2
            kv_t = jnp.concatenate([_dot_tn(v_c[:, :half], k_c), _dot_tn(v_c[:, half:], k_c)], axis=0)
            st_ref[r] = st * decay[r][r0:r0 + 1, :] + jnp.where(smask_ref[...] > 0.5, kv_t, 0.0)
    o_inter = [jnp.concatenate(o, axis=0) for o in o_inter]

    causal = tri_ref[...] > 0.5
    pair_w = 2 * GLA_DV
    first_head = lax.broadcasted_iota(jnp.int32, (1, pair_w), 1) < GLA_DV
    for pair in range(GLA_HEADS // 2):
        lo = pair * pair_w
        for r in seqs:
            scores = []
            for hd in (2 * pair, 2 * pair + 1):
                q_h = jnp.where(hmask_ref[hd:hd + 1, :] > 0.5, q_in[r], jnp.zeros_like(q_in[r]))
                scores.append(jnp.where(causal, _dot_nt(q_h, k_in[r]), 0.0).astype(BF16))
            v_pair = v[r][:, lo:lo + pair_w]
            v_diag = jnp.concatenate([jnp.where(first_head, v_pair, jnp.zeros_like(v_pair)),
                                      jnp.where(first_head, jnp.zeros_like(v_pair), v_pair)], axis=0)
            o_pair = _dot(jnp.concatenate(scores, axis=1), v_diag) + o_inter[r][:, lo:lo + pair_w]
            for hd in (2 * pair, 2 * pair + 1):
                off = (hd - 2 * pair) * GLA_DV
                g_h = projs[r][:, _GLA_GO + hd * GLA_DV:_GLA_GO + (hd + 1) * GLA_DV]
                y_h = _rms(o_pair[:, off:off + GLA_DV], ogain_ref[...]) * (g_h / (1.0 + jnp.exp(-g_h)))
                cat_ref[r, :, hd * GLA_DV:(hd + 1) * GLA_DV] = y_h.astype(BF16)

    for r in seqs:
        o_ref[r] = xs[r] + _dot(cat_ref[r], wout_ref[...])

    @pl.when(pl.program_id(1) + 1 < pl.num_programs(1))
    def _():
        for r in seqs:
            utail_ref[r] = ubuf_ref[r, tm:tm + SUBLANES, :]


def _gla_layer(x, gain, w_in, w_gate2, gate_bias, out_gain, conv_w, w_out):
    b, t, d = x.shape
    tm = min(GLA_TM, t)
    rows = GLA_ROWS
    widths = [GLA_KW, GLA_KW, GLA_VW, GLA_RANK, GLA_VW, CONV_DIM, CONV_DIM, CONV_DIM]
    cuts = [int(c) for c in np.cumsum(widths)[:-1]]
    wq, wk, wv, wgl, wgo, wcb, wcc, wci = jnp.split(w_in, cuts, axis=1)
    wgl = jnp.pad(wgl, ((0, 0), (0, RANK_PAD - GLA_RANK)))
    w_in_r = jnp.concatenate([wq, wk, wv, wgo, wcb, wcc, wci, wgl], axis=1).astype(BF16)
    wg2 = jnp.pad(w_gate2, ((0, RANK_PAD - GLA_RANK), (0, 0))).astype(BF16)

    r = np.arange(tm)
    same_chunk = (r[:, None] // GLA_CHUNK) == (r[None, :] // GLA_CHUNK)
    tri = jnp.asarray(same_chunk & (r[None, :] <= r[:, None]), BF16)
    hmask = jnp.asarray(np.arange(GLA_KW)[None, :] // GLA_DK == np.arange(GLA_HEADS)[:, None], F32)
    smask = jnp.asarray(
        (np.arange(GLA_VW)[:, None] // GLA_DV) == (np.arange(GLA_KW)[None, :] // GLA_DK), F32)

    return pl.pallas_call(
        _gla_body,
        out_shape=jax.ShapeDtypeStruct((b, t, d), F32),
        grid=(b // rows, t // tm),
        in_specs=[
            pl.BlockSpec((rows, tm, d), lambda i, j: (i, j, 0)),
            _const_spec((1, d)),
            _const_spec((d, GLA_PROJ_W)),
            _const_spec((RANK_PAD, GLA_KW)),
            _const_spec((1, GLA_KW)),
            _const_spec((1, GLA_DV)),
            _const_spec((CONV_WIDTH, CONV_DIM)),
            _const_spec((GLA_VW + CONV_DIM, d)),
            _const_spec((tm, tm)),
            _const_spec((GLA_HEADS, GLA_KW)),
            _const_spec((GLA_VW, GLA_KW)),
        ],
        out_specs=pl.BlockSpec((rows, tm, d), lambda i, j: (i, j, 0)),
        scratch_shapes=[
            pltpu.VMEM((rows, GLA_VW, GLA_KW), F32),
            pltpu.VMEM((rows, tm + SUBLANES, CONV_DIM), F32),
            pltpu.VMEM((rows, SUBLANES, CONV_DIM), F32),
            pltpu.VMEM((rows, tm, GLA_VW + CONV_DIM), BF16),
        ],
        compiler_params=_params("parallel", "arbitrary"),
        name="gla_conv",
    )(x, gain.reshape(1, d), w_in_r, wg2, gate_bias.reshape(1, GLA_KW), out_gain.reshape(1, GLA_DV),
      conv_w, w_out.astype(BF16), tri, hmask, smask)


def _norm_rope_heads(z, n_heads, gain, cos_f, sin_f):
    hd, half = SWA_HEAD_DIM, ROT_DIM // 2
    out = []
    for head in range(n_heads):
        zh = z[head * hd:(head + 1) * hd, :]
        ss = jnp.sum(zh * zh, axis=0, keepdims=True)
        zh = zh * lax.rsqrt(ss * (1.0 / hd) + RMS_EPS) * gain
        x1, x2 = zh[0:half], zh[half:ROT_DIM]
        out.append(jnp.concatenate(
            [x1 * cos_f - x2 * sin_f, x2 * cos_f + x1 * sin_f, zh[ROT_DIM:]], axis=0))
    return out


def _swa_body(x_ref, gain_ref, wqkv_t_ref, qgain_ref, kgain_ref, cos_ref, sin_ref,
              band_ref, band0_ref, sink_ref, wout_ref, o_ref,
              qcat_ref, kbuf_ref, vbuf_ref, kprev_ref, vprev_ref, att_ref):
    n_seq, tq = x_ref.shape[0], x_ref.shape[1]
    seqs = range(n_seq)
    nb = tq // WINDOW
    n = pl.program_id(1)
    hd = SWA_HEAD_DIM
    qw = SWA_HEADS * hd
    kw = SWA_KV_HEADS * hd

    @pl.when(n == 0)
    def _():
        kprev_ref[...] = jnp.zeros_like(kprev_ref)
        vprev_ref[...] = jnp.zeros_like(vprev_ref)

    xs = [x_ref[r] for r in seqs]
    qkv_t = [_dot_nt(wqkv_t_ref[...], _rms(x, gain_ref[...]).astype(BF16)) for x in xs]

    for r in seqs:
        cos_f, sin_f = cos_ref[r], sin_ref[r]
        k_heads = _norm_rope_heads(qkv_t[r][qw:qw + kw], SWA_KV_HEADS, kgain_ref[...], cos_f, sin_f)
        kbuf_ref[r, 0:WINDOW, :] = kprev_ref[r]
        kbuf_ref[r, WINDOW:WINDOW + tq, :] = jnp.concatenate(k_heads, axis=0).T.astype(BF16)
        vbuf_ref[r, :, 0:WINDOW] = vprev_ref[r]
        vbuf_ref[r, :, WINDOW:WINDOW + tq] = qkv_t[r][qw + kw:, :].astype(BF16)
        q_heads = _norm_rope_heads(qkv_t[r][:qw], SWA_HEADS, qgain_ref[...], cos_f, sin_f)
        for head, z in enumerate(q_heads):
            g, j = divmod(head, SWA_GROUP)
            z = z.astype(BF16)
            for i in range(nb):
                qcat_ref[r, g, i, :, j * WINDOW:(j + 1) * WINDOW] = (
                    z[:, i * WINDOW:(i + 1) * WINDOW])

    bias0 = jnp.where(n > 0, band_ref[...], band0_ref[...])

    units = [(r, g, i) for g in range(SWA_KV_HEADS) for i in range(nb) for r in seqs]

    def scores(unit):
        r, g, i = unit
        return _dot(kbuf_ref[r, i * WINDOW:(i + 2) * WINDOW, g * hd:(g + 1) * hd], qcat_ref[r, g, i])

    ones_rows = jnp.ones((2 * SUBLANES, 2 * WINDOW), BF16)
    waves = [units[w:w + SWA_WAVE] for w in range(0, len(units), SWA_WAVE)]
    pending = [[scores(unit) for unit in wave] for wave in waves[:SWA_LOOKAHEAD]]
    for w, wave in enumerate(waves):
        s_wave = pending.pop(0)
        if w + SWA_LOOKAHEAD < len(waves):
            pending.append([scores(unit) for unit in waves[w + SWA_LOOKAHEAD]])
        stats, probs = [], []
        for (r, g, i), s in zip(wave, s_wave):
            s = s + (bias0 if i == 0 else band_ref[...])
            sink = sink_ref[g]
            m = jnp.maximum(jnp.max(s, axis=0, keepdims=True), sink)
            if w + SWA_LOOKAHEAD < len(waves):
                m = m + jnp.minimum(jnp.abs(pending[-1][0][0:1, :]), 0.0)
            probs.append(jnp.exp2(s - m).astype(BF16))
            stats.append((sink, m))
        for (r, g, i), p, (sink, m) in zip(wave, probs, stats):
            r0 = i * WINDOW
            v_aug = jnp.concatenate(
                [vbuf_ref[r, g * hd:(g + 1) * hd, r0:r0 + 2 * WINDOW], ones_rows], axis=0)
            o_aug = _dot(v_aug, p)
            denom = o_aug[hd:hd + 1, :] + jnp.exp2(sink - m)
            o = o_aug[0:hd, :] * (1.0 / denom)
            for j in range(SWA_GROUP):
                head = g * SWA_GROUP + j
                att_ref[r, head * hd:(head + 1) * hd, r0:r0 + WINDOW] = (
                    o[:, j * WINDOW:(j + 1) * WINDOW].astype(BF16))

    for r in seqs:
        o_ref[r] = xs[r] + _dot_tn(att_ref[r], wout_ref[...])

    @pl.when(n + 1 < pl.num_programs(1))
    def _():
        for r in seqs:
            kprev_ref[r] = kbuf_ref[r, tq:tq + WINDOW, :]
            vprev_ref[r] = vbuf_ref[r, :, tq:tq + WINDOW]


def _swa_layer(x, positions, gain, w_qkv, q_gain, k_gain, sinks, w_out):
    b, t, d = x.shape
    tq = min(SWA_TQ, t)
    nb = tq // WINDOW
    hd = SWA_HEAD_DIM
    qw = SWA_HEADS * hd
    kw = SWA_KV_HEADS * hd
    inv_freq = ROPE_THETA ** (-jnp.arange(0, ROT_DIM, 2, dtype=F32) / ROT_DIM)
    ang = positions.astype(F32)[:, None, :] * inv_freq[None, :, None]
    cos_f, sin_f = jnp.cos(ang), jnp.sin(ang)
    log2e = float(np.log2(np.e))
    qgain_f = jnp.broadcast_to((q_gain * (hd ** -0.5 * log2e))[:, None], (hd, tq))
    kgain_f = jnp.broadcast_to(k_gain[:, None], (hd, tq))
    rows = SWA_GROUP * WINDOW
    si = np.arange(2 * WINDOW)[:, None]
    qi = np.arange(WINDOW)[None, :] + WINDOW
    band_ok = (qi - si >= 0) & (qi - si < WINDOW)
    band = jnp.asarray(np.tile(np.where(band_ok, 0.0, -np.inf), (1, SWA_GROUP)), F32)
    band0 = jnp.asarray(np.tile(np.where(band_ok & (si >= WINDOW), 0.0, -np.inf), (1, SWA_GROUP)), F32)
    sink_cols = jnp.repeat((sinks.astype(F32) * log2e).reshape(SWA_KV_HEADS, 1, SWA_GROUP), WINDOW, axis=2)

    n_seq = SWA_ROWS
    row_spec = pl.BlockSpec((n_seq, tq, d), lambda i, j: (i, j, 0))
    tab_spec = pl.BlockSpec((n_seq, ROT_DIM // 2, tq), lambda i, j: (i, 0, j))
    return pl.pallas_call(
        _swa_body,
        out_shape=jax.ShapeDtypeStruct((b, t, d), F32),
        grid=(b // n_seq, t // tq),
        in_specs=[
            row_spec,
            _const_spec((1, d)),
            _const_spec((qw + 2 * kw, d)),
            _const_spec((hd, tq)),
            _const_spec((hd, tq)),
            tab_spec, tab_spec,
            _const_spec((2 * WINDOW, rows)),
            _const_spec((2 * WINDOW, rows)),
            _const_spec((SWA_KV_HEADS, 1, rows)),
            _const_spec((d, d)),
        ],
        out_specs=row_spec,
        scratch_shapes=[
            pltpu.VMEM((n_seq, SWA_KV_HEADS, nb, hd, rows), BF16),
            pltpu.VMEM((n_seq, WINDOW + tq, kw), BF16),
            pltpu.VMEM((n_seq, kw, WINDOW + tq), BF16),
            pltpu.VMEM((n_seq, WINDOW, kw), BF16),
            pltpu.VMEM((n_seq, kw, WINDOW), BF16),
            pltpu.VMEM((n_seq, qw, tq), BF16),
        ],
        compiler_params=_params("parallel", "arbitrary"),
        name="swa",
    )(x, gain.reshape(1, d), w_qkv.T.astype(BF16), qgain_f, kgain_f, cos_f, sin_f,
      band, band0, sink_cols, w_out.astype(BF16))


def kernel(x, mem, positions, mix_norm, hyb_w_in, gla_w_gate2, gla_gate_bias, gla_out_gain, conv_w,
           hyb_w_out, swa_w_qkv, swa_q_gain, swa_k_gain, swa_sinks, swa_w_out, mem_norm, xa_norm, xa_wq,
           xa_wkv, xa_q_gain, xa_k_gain, xa_wo, ffn_norm, ffn_w_gate_up, ffn_w_down):
    b, t, d = x.shape
    depth = mix_norm.shape[0]
    xa_wq_t = jnp.swapaxes(xa_wq, 1, 2).astype(BF16)
    xa_wkv_b, xa_wo_b = xa_wkv.astype(BF16), xa_wo.astype(BF16)
    ffn_wgu_b, ffn_wd_b = ffn_w_gate_up.astype(BF16), ffn_w_down.astype(BF16)
    for layer in range(depth):
        i = layer // 2
        if layer % 2 == 0:
            x = _gla_layer(x, mix_norm[layer], hyb_w_in[i], gla_w_gate2[i], gla_gate_bias[i],
                           gla_out_gain[i], conv_w[i], hyb_w_out[i])
        else:
            x = _swa_layer(x, positions, mix_norm[layer], swa_w_qkv[i], swa_q_gain[i], swa_k_gain[i],
                           swa_sinks[i], swa_w_out[i])
        x = _cross_attention(x, mem, layer, mem_norm, xa_norm, xa_wq_t, xa_wkv_b, xa_q_gain, xa_k_gain,
                             xa_wo_b)
        x = _ffn(x.reshape(b * t, d), layer, ffn_norm, ffn_wgu_b, ffn_wd_b).reshape(b, t, d)
    return x
```
